```python
import math
import jax, jax.numpy as jnp
from jax import lax
import numpy as np

D_MODEL = 1024
BATCH = 4
SEQ = 8192
DEPTH = 4

HEAD_DIM = 64
N_BRANCH = 4
BRANCH_WIDTH = D_MODEL // N_BRANCH
POOL_WINDOWS = (2, 4, 8, 16)
N_POOL_GROUPS = len(POOL_WINDOWS)
POOL_GROUP = BRANCH_WIDTH // N_POOL_GROUPS
CONV_K = 3
MOBA_HEADS = BRANCH_WIDTH // HEAD_DIM
MOBA_BLOCK = 256
MOBA_TOPK = 3
MOBA_Q_CHUNK = 64
FOX_HEADS = BRANCH_WIDTH // HEAD_DIM
FOX_Q_BLOCK = 128
D_FF = 4 * D_MODEL
ROPE_THETA = 500000.0
ROPE_DIM = HEAD_DIM // 4
ALPHA = (2 * DEPTH) ** 0.25
BETA = (8 * DEPTH) ** -0.25
LN_EPS = 1e-5
FORGET_BIAS_INIT = 2.0
IN_SPLIT_WIDTHS = (BRANCH_WIDTH,) * 10 + (FOX_HEADS, N_BRANCH * D_MODEL)
IN_WIDTH = 10 * BRANCH_WIDTH + FOX_HEADS + N_BRANCH * D_MODEL
FORGET_COL0 = 10 * BRANCH_WIDTH

kernel_name = "hybrid_parallel_pool_conv_moba_fox_block"


def _in_split_points():
    pts, acc = [], 0
    for w in IN_SPLIT_WIDTHS[:-1]:
        acc += w
        pts.append(acc)
    return pts


def layer_norm(x):
    xf = x.astype(jnp.float32)
    mu = jnp.mean(xf, axis=-1, keepdims=True)
    var = jnp.mean(jnp.square(xf - mu), axis=-1, keepdims=True)
    return ((xf - mu) * lax.rsqrt(var + LN_EPS)).astype(x.dtype)


def rotary_tables(positions):
    inv = ROPE_THETA ** (-jnp.arange(0, ROPE_DIM, 2, dtype=jnp.float32) / ROPE_DIM)
    ang = positions.astype(jnp.float32)[..., None] * inv
    return jnp.cos(ang)[:, None], jnp.sin(ang)[:, None]


def apply_partial_rotary(t, cos, sin):
    half = ROPE_DIM // 2
    t1, t2, rest = t[..., :half], t[..., half:ROPE_DIM], t[..., ROPE_DIM:]
    r1 = (t1 * cos - t2 * sin).astype(t.dtype)
    r2 = (t2 * cos + t1 * sin).astype(t.dtype)
    return jnp.concatenate([r1, r2, rest], axis=-1)


def to_heads(t, n_heads):
    b, s, _ = t.shape
    return t.reshape(b, s, n_heads, HEAD_DIM).transpose(0, 2, 1, 3)


def from_heads(t):
    b, h, s, d = t.shape
    return t.transpose(0, 2, 1, 3).reshape(b, s, h * d)


def pool_mixer(p, w_pool, pool_scale):
    s = p.shape[1]
    t = jnp.arange(s)
    groups = jnp.split(p, N_POOL_GROUPS, axis=-1)
    outs = []
    for g, w in zip(groups, POOL_WINDOWS):
        cs = jnp.cumsum(g.astype(jnp.float32), axis=1)
        cs_prev = jnp.pad(cs, ((0, 0), (w, 0), (0, 0)))[:, :s]
        cnt = jnp.minimum(t + 1, w).astype(jnp.float32)[None, :, None]
        outs.append(((cs - cs_prev) / cnt).astype(p.dtype) - g)
    pooled = jnp.stack(outs, axis=2)
    mixed = jnp.einsum('bsgc,gcd->bsgd', pooled, w_pool)
    return mixed.reshape(p.shape) * pool_scale


def short_conv_mixer(gate_b, gate_c, h, conv_w):
    v = gate_c * h
    z = lax.conv_general_dilated(
        v, conv_w[:, None, :], window_strides=(1,), padding=[(CONV_K - 1, 0)],
        dimension_numbers=('NWC', 'WIO', 'NWC'), feature_group_count=v.shape[-1])
    return gate_b * z


def moba_attention(q, k, v):
    b, h, s, hd = q.shape
    nblk = -(-s // MOBA_BLOCK)
    s_pad = nblk * MOBA_BLOCK
    pad = ((0, 0), (0, 0), (0, s_pad - s), (0, 0))
    kb = jnp.pad(k, pad).reshape(b, h, nblk, MOBA_BLOCK, hd)
    vb = jnp.pad(v, pad).reshape(b, h, nblk, MOBA_BLOCK, hd)
    kbar = jnp.mean(kb.astype(jnp.float32), axis=3)
    n_sel = min(MOBA_TOPK, nblk)
    scale = HEAD_DIM ** -0.5
    bi = jnp.arange(b)[:, None, None, None]
    hi = jnp.arange(h)[None, :, None, None]
    blk_ids = jnp.arange(nblk)
    n_chunks = s // MOBA_Q_CHUNK

    def chunk(ci):
        t0 = ci * MOBA_Q_CHUNK
        qc = lax.dynamic_slice_in_dim(q, t0, MOBA_Q_CHUNK, axis=2)
        tq = t0 + jnp.arange(MOBA_Q_CHUNK)
        own = t0 // MOBA_BLOCK
        score = jnp.einsum('bhqd,bhnd->bhqn', qc.astype(jnp.float32), kbar)
        score = jnp.where(blk_ids < own, score, -jnp.inf)
        _, idx = lax.top_k(score, n_sel)
        sel_valid = idx < own
        k_sel = kb[bi, hi, idx]
        v_sel = vb[bi, hi, idx]
        l_sel = jnp.einsum('bhqd,bhqkld->bhqkl', qc, k_sel).astype(jnp.float32) * scale
        l_sel = jnp.where(sel_valid[..., None], l_sel, -jnp.inf)
        k_own = lax.dynamic_index_in_dim(kb, own, axis=2, keepdims=False)
        v_own = lax.dynamic_index_in_dim(vb, own, axis=2, keepdims=False)
        l_own = jnp.einsum('bhqd,bhld->bhql', qc, k_own).astype(jnp.float32) * scale
        pos_own = own * MOBA_BLOCK + jnp.arange(MOBA_BLOCK)
        l_own = jnp.where(pos_own[None, :] <= tq[:, None], l_own, -jnp.inf)
        n_k = n_sel * MOBA_BLOCK
        logits = jnp.concatenate([l_sel.reshape(b, h, MOBA_Q_CHUNK, n_k), l_own], axis=-1)
        p = jax.nn.softmax(logits, axis=-1)
        p_sel = p[..., :n_k].reshape(b, h, MOBA_Q_CHUNK, n_sel, MOBA_BLOCK).astype(v.dtype)
        p_own = p[..., n_k:].astype(v.dtype)
        return (jnp.einsum('bhqkl,bhqkld->bhqd', p_sel, v_sel)
                + jnp.einsum('bhql,bhld->bhqd', p_own, v_own))

    out = lax.map(chunk, jnp.arange(n_chunks))
    return out.transpose(1, 2, 0, 3, 4).reshape(b, h, s, hd)


def forgetting_attention(q, k, v, log_f):
    b, h, s, hd = q.shape
    cum = jnp.cumsum(log_f, axis=-1)
    scale = HEAD_DIM ** -0.5
    s_pos = jnp.arange(s)

    def block(qi):
        t0 = qi * FOX_Q_BLOCK
        qb = lax.dynamic_slice_in_dim(q, t0, FOX_Q_BLOCK, axis=2)
        cq = lax.dynamic_slice_in_dim(cum, t0, FOX_Q_BLOCK, axis=2)
        tq = t0 + jnp.arange(FOX_Q_BLOCK)
        logits = (jnp.einsum('bhqd,bhkd->bhqk', qb, k).astype(jnp.float32) * scale
                  + cq[..., :, None] - cum[..., None, :])
        logits = jnp.where(s_pos[None, :] <= tq[:, None], logits, -jnp.inf)
        p = jax.nn.softmax(logits, axis=-1)
        return jnp.einsum('bhqk,bhkd->bhqd', p.astype(v.dtype), v)

    out = lax.map(block, jnp.arange(s // FOX_Q_BLOCK))
    return out.transpose(1, 2, 0, 3, 4).reshape(b, h, s, hd)


def hybrid_layer(x, cond_act, cos, sin, w_ada, b_ada, w_in, b_in, w_pool, pool_scale, conv_w,
                 w_branch, w_o, ln1_g, ln1_b, w_up, b_up, w_down, ln2_g, ln2_b):
    ada = (cond_act @ w_ada + b_ada)[:, None, :]
    sh1, sc1, g1, sh2, sc2, g2 = jnp.split(ada, 6, axis=-1)

    u = layer_norm(x) * (1 + sc1) + sh1
    proj = u @ w_in + b_in
    (p_in, c_b, c_c, c_h, mq, mk, mv, fq, fk, fv, f_logit, gate_logit) = jnp.split(
        proj, _in_split_points(), axis=-1)

    y_pool = pool_mixer(p_in, w_pool, pool_scale)
    y_conv = short_conv_mixer(c_b, c_c, c_h, conv_w)
    q_m = apply_partial_rotary(to_heads(mq, MOBA_HEADS), cos, sin)
    k_m = apply_partial_rotary(to_heads(mk, MOBA_HEADS), cos, sin)
    y_moba = from_heads(moba_attention(q_m, k_m, to_heads(mv, MOBA_HEADS)))
    log_f = jax.nn.log_sigmoid(f_logit.astype(jnp.float32)).transpose(0, 2, 1)
    y_fox = from_heads(forgetting_attention(to_heads(fq, FOX_HEADS), to_heads(fk, FOX_HEADS),
                                            to_heads(fv, FOX_HEADS), log_f))

    gates = jax.nn.sigmoid(gate_logit)
    branches = (y_pool, y_conv, y_moba, y_fox)
    merged = None
    for n in range(N_BRANCH):
        term = gates[..., n * D_MODEL:(n + 1) * D_MODEL] * (branches[n] @ w_branch[n])
        merged = term if merged is None else merged + term
    mixed = merged @ w_o
    x = layer_norm(ALPHA * x + g1 * mixed) * ln1_g + ln1_b

    u2 = layer_norm(x) * (1 + sc2) + sh2
    hdn = jnp.square(jax.nn.relu(u2 @ w_up + b_up))
    x = layer_norm(ALPHA * x + g2 * (hdn @ w_down)) * ln2_g + ln2_b
    return x


def setup_inputs(seed: int = 0) -> dict:
    key = jax.random.key(seed)
    ks = jax.random.split(key, 24)
    f32 = jnp.float32
    nrm = lambda k, shape, s: jax.random.normal(k, shape, f32) * s
    x = jax.random.normal(ks[0], (BATCH, SEQ, D_MODEL), f32)
    c = jax.random.normal(ks[1], (BATCH, D_MODEL), f32)
    offsets = jax.random.randint(ks[2], (BATCH, 1), 0, 1024, dtype=jnp.int32)
    positions = jnp.arange(SEQ, dtype=jnp.int32)[None, :] + offsets
    w_ada = nrm(ks[3], (DEPTH, D_MODEL, 6 * D_MODEL), D_MODEL ** -0.5)
    b_ada = nrm(ks[4], (DEPTH, 6 * D_MODEL), 0.01)
    w_in = nrm(ks[5], (DEPTH, D_MODEL, IN_WIDTH), D_MODEL ** -0.5)
    b_in_raw = nrm(ks[6], (DEPTH, IN_WIDTH), 0.01)
    forget_b = FORGET_BIAS_INIT + nrm(ks[7], (DEPTH, FOX_HEADS), 0.1)
    b_in = jnp.concatenate([b_in_raw[:, :FORGET_COL0], forget_b,
                            b_in_raw[:, FORGET_COL0 + FOX_HEADS:]], axis=-1)
    w_pool = nrm(ks[8], (DEPTH, N_POOL_GROUPS, POOL_GROUP, POOL_GROUP), POOL_GROUP ** -0.5)
    pool_scale = 1.0 + nrm(ks[9], (DEPTH, BRANCH_WIDTH), 0.02)
    conv_w = nrm(ks[10], (DEPTH, CONV_K, BRANCH_WIDTH), CONV_K ** -0.5)
    w_branch = nrm(ks[11], (DEPTH, N_BRANCH, BRANCH_WIDTH, D_MODEL), BRANCH_WIDTH ** -0.5)
    w_o = nrm(ks[12], (DEPTH, D_MODEL, D_MODEL), BETA * D_MODEL ** -0.5)
    ln1_g = 1.0 + nrm(ks[13], (DEPTH, D_MODEL), 0.02)
    ln1_b = nrm(ks[14], (DEPTH, D_MODEL), 0.02)
    w_up = nrm(ks[15], (DEPTH, D_MODEL, D_FF), D_MODEL ** -0.5)
    b_up = nrm(ks[16], (DEPTH, D_FF), 0.01)
    w_down = nrm(ks[17], (DEPTH, D_FF, D_MODEL), BETA * D_FF ** -0.5)
    ln2_g = 1.0 + nrm(ks[18], (DEPTH, D_MODEL), 0.02)
    ln2_b = nrm(ks[19], (DEPTH, D_MODEL), 0.02)
    return {"x": x, "c": c, "positions": positions, "w_ada": w_ada, "b_ada": b_ada,
            "w_in": w_in, "b_in": b_in, "w_pool": w_pool, "pool_scale": pool_scale,
            "conv_w": conv_w, "w_branch": w_branch, "w_o": w_o, "ln1_g": ln1_g, "ln1_b": ln1_b,
            "w_up": w_up, "b_up": b_up, "w_down": w_down, "ln2_g": ln2_g, "ln2_b": ln2_b}


def reference(x, c, positions, w_ada, b_ada, w_in, b_in, w_pool, pool_scale, conv_w, w_branch,
              w_o, ln1_g, ln1_b, w_up, b_up, w_down, ln2_g, ln2_b):
    cos, sin = rotary_tables(positions)
    cond_act = jax.nn.silu(c)
    for l in range(DEPTH):
        x = hybrid_layer(x, cond_act, cos, sin, w_ada[l], b_ada[l], w_in[l], b_in[l], w_pool[l],
                         pool_scale[l], conv_w[l], w_branch[l], w_o[l], ln1_g[l], ln1_b[l],
                         w_up[l], b_up[l], w_down[l], ln2_g[l], ln2_b[l])
    return x
```

```python
import functools

import jax
import jax.numpy as jnp
from jax import lax
from jax.experimental import pallas as pl
from jax.experimental.pallas import tpu as pltpu

F32 = jnp.float32
BF16 = jnp.bfloat16

LANES = 128
HEAD_DIM = 64
N_HEADS = 4
BRANCH = N_HEADS * HEAD_DIM
POOL_WINDOWS = (2, 4, 8, 16)
POOL_HALO = 16
CONV_HALO = 8
MOBA_BLOCK = 256
MOBA_TOPK = 3
ROPE_THETA = 500000.0
ROPE_DIM = HEAD_DIM // 4
LN_EPS = 1e-5
LOG2E = 1.4426950408889634
QSCALE = HEAD_DIM ** -0.5 * LOG2E
MASK = -1e30
NEG = -3e38
N_MAIN = 10 * BRANCH
VMEM_LIMIT = 48 * 1024 * 1024

_NT = (((1,), (1,)), ((), ()))


def _params(n_axes):
    return pltpu.CompilerParams(dimension_semantics=("arbitrary",) * n_axes,
                                vmem_limit_bytes=VMEM_LIMIT)


def _const_spec(shape):
    nd = len(shape)
    return pl.BlockSpec(shape, lambda *_: (0,) * nd, pipeline_mode=pl.Buffered(1))


def _layer_norm(x):
    mu = jnp.mean(x, axis=-1, keepdims=True)
    xc = x - mu
    var = jnp.mean(xc * xc, axis=-1, keepdims=True)
    return xc * lax.rsqrt(var + LN_EPS)


def _split3(x):
    hi = x.astype(BF16)
    r1 = x - hi.astype(F32)
    mid = r1.astype(BF16)
    lo = (r1 - mid.astype(F32)).astype(BF16)
    return hi, mid, lo


def _rope_kernel(pos_ref, inv_ref, c_ref, s1_ref, s2_ref):
    ang = pos_ref[0].astype(F32) * inv_ref[...]
    cos, sin = jnp.cos(ang), jnp.sin(ang)
    d = lax.broadcasted_iota(jnp.int32, ang.shape, 1) & (HEAD_DIM - 1)
    half = ROPE_DIM // 2
    c_ref[0] = jnp.where(d < ROPE_DIM, cos, 1.0)
    s1_ref[0] = jnp.where(d < half, -sin, 0.0)
    s2_ref[0] = jnp.where((d >= half) & (d < ROPE_DIM), sin, 0.0)


def _rope_tables(positions, ts):
    b, s = positions.shape
    half = ROPE_DIM // 2
    inv = ROPE_THETA ** (-jnp.arange(0, ROPE_DIM, 2, dtype=F32) / ROPE_DIM)
    inv_row = jnp.tile(inv, LANES // half)[None, :]
    tab = jax.ShapeDtypeStruct((b, s, LANES), F32)
    spec = pl.BlockSpec((1, ts, LANES), lambda i, j: (i, j, 0))
    return pl.pallas_call(
        _rope_kernel, grid=(b, s // ts),
        in_specs=[pl.BlockSpec((1, ts, 1), lambda i, j: (i, j, 0)), _const_spec((1, LANES))],
        out_specs=[spec, spec, spec], out_shape=[tab, tab, tab],
        compiler_params=_params(2), name="rope_tables",
    )(positions[..., None], inv_row)


def _ada_kernel(c_ref, w_ref, b_ref, o_ref):
    c = c_ref[...]
    act = c * jax.nn.sigmoid(c)
    o_ref[...] = jnp.dot(act, w_ref[...], preferred_element_type=F32) + b_ref[...]


def _ada_all(c, w_ada, b_ada, tn=2048):
    depth, d, n = w_ada.shape
    b = c.shape[0]
    rows = -(-b // 8) * 8
    c_pad = jnp.pad(c, ((0, rows - b), (0, 0)))
    out = pl.pallas_call(
        _ada_kernel, grid=(depth, n // tn),
        in_specs=[pl.BlockSpec((rows, d), lambda l, j: (0, 0)),
                  pl.BlockSpec((None, d, tn), lambda l, j: (l, 0, j)),
                  pl.BlockSpec((None, 1, tn), lambda l, j: (l, 0, j))],
        out_specs=pl.BlockSpec((None, rows, tn), lambda l, j: (l, 0, j)),
        out_shape=jax.ShapeDtypeStruct((depth, rows, n), F32),
        compiler_params=_params(2), name="ada_cond",
    )(c_pad, w_ada, b_ada[:, None, :])
    return out[:, :b].reshape(depth, b, 6, d)


def _inproj_kernel(x_ref, ada_ref, w_ref, b_ref, wpool_ref, pscale_ref, convw_ref,
                   rc_ref, rs1_ref, rs2_ref, tri_ref,
                   u_ref, ypool_ref, yconv_ref, qm_ref, km_ref, vm_ref, qf_ref, kf_ref, vf_ref,
                   pbuf, vbuf, ccarry, *, ts):
    t = pl.program_id(1)

    @pl.when(t == 0)
    def _():
        pbuf[0:POOL_HALO, :] = jnp.zeros((POOL_HALO, BRANCH), F32)
        vbuf[0:CONV_HALO, :] = jnp.zeros((CONV_HALO, BRANCH), F32)
        ccarry[...] = jnp.zeros_like(ccarry)

    u = _layer_norm(x_ref[0]) * (1.0 + ada_ref[0, 1:2, :]) + ada_ref[0, 0:1, :]
    ub = u.astype(BF16)
    u_ref[0] = ub

    def proj(lo, hi):
        return jnp.dot(ub, w_ref[:, lo:hi], preferred_element_type=F32) + b_ref[:, lo:hi]

    lane = lax.broadcasted_iota(jnp.int32, (ts, LANES), 1)
    row = lax.broadcasted_iota(jnp.int32, (ts, 1), 0)
    first = lane < HEAD_DIM

    p_in = proj(0, BRANCH)
    pbuf[POOL_HALO:POOL_HALO + ts, :] = p_in
    posp1 = (t * ts + row + 1).astype(F32)

    def window_sum(lo, w):
        acc = pbuf[POOL_HALO:POOL_HALO + ts, lo:lo + LANES]
        for j in range(1, w):
            acc = acc + pbuf[POOL_HALO - j:POOL_HALO - j + ts, lo:lo + LANES]
        return acc

    pooled = []
    for pair in range(2):
        w_a, w_b = POOL_WINDOWS[2 * pair], POOL_WINDOWS[2 * pair + 1]
        lo = pair * LANES
        win = jnp.where(first, window_sum(lo, w_a), window_sum(lo, w_b))
        cnt = jnp.where(first, jnp.minimum(posp1, float(w_a)), jnp.minimum(posp1, float(w_b)))
        pooled.append(win / cnt - p_in[:, lo:lo + LANES])
    pooled = jnp.concatenate(pooled, axis=-1).astype(BF16)
    y_pool = jnp.dot(pooled, wpool_ref[...], preferred_element_type=F32) * pscale_ref[...]
    ypool_ref[0] = y_pool.astype(BF16)
    pbuf[0:POOL_HALO, :] = pbuf[ts:ts + POOL_HALO, :]

    conv = proj(BRANCH, 4 * BRANCH)
    v = conv[:, BRANCH:2 * BRANCH] * conv[:, 2 * BRANCH:3 * BRANCH]
    vbuf[CONV_HALO:CONV_HALO + ts, :] = v
    z = (convw_ref[2:3, :] * v
         + convw_ref[1:2, :] * vbuf[CONV_HALO - 1:CONV_HALO - 1 + ts, :]
         + convw_ref[0:1, :] * vbuf[CONV_HALO - 2:CONV_HALO - 2 + ts, :])
    yconv_ref[0] = (conv[:, 0:BRANCH] * z).astype(BF16)
    vbuf[0:CONV_HALO, :] = vbuf[ts:ts + CONV_HALO, :]

    rc, rs1, rs2 = rc_ref[0], rs1_ref[0], rs2_ref[0]
    half = ROPE_DIM // 2

    def rope_pairs(a):
        out = []
        for pair in range(2):
            ah = a[:, pair * LANES:(pair + 1) * LANES]
            out.append(ah * rc + pltpu.roll(ah, LANES - half, 1) * rs1 + pltpu.roll(ah, half, 1) * rs2)
        return out

    def plain_pairs(a):
        return [a[:, 0:LANES], a[:, LANES:2 * LANES]]

    def head(pairs, h):
        a = pairs[h // 2]
        return pltpu.roll(a, HEAD_DIM, 1) if h % 2 else a

    ones_col = jnp.where(lane == HEAD_DIM, 1.0, 0.0)

    moba = proj(4 * BRANCH, 7 * BRANCH)
    mq = rope_pairs(moba[:, 0:BRANCH])
    mk = rope_pairs(moba[:, BRANCH:2 * BRANCH])
    mv = plain_pairs(moba[:, 2 * BRANCH:3 * BRANCH])
    blk = (t * ts + row) // MOBA_BLOCK
    blk_onehot = jnp.where(lane == HEAD_DIM + blk, 1.0, 0.0)
    for h in range(N_HEADS):
        qm_ref[0, h] = jnp.where(first, head(mq, h) * QSCALE, 0.0).astype(BF16)
        km_ref[0, h] = jnp.where(first, head(mk, h), blk_onehot).astype(BF16)
        vm_ref[0, h] = jnp.where(first, head(mv, h), ones_col).astype(BF16)

    fl = proj(N_MAIN, N_MAIN + LANES)
    log_f = jnp.minimum(fl, 0.0) - jnp.log1p(jnp.exp(-jnp.abs(fl)))
    hi, mid, lo = _split3(log_f)
    zero = jnp.zeros_like(hi)
    pieces = jnp.where(lane < N_HEADS, hi, jnp.where(lane < 2 * N_HEADS, mid,
                                                      jnp.where(lane < 3 * N_HEADS, lo, zero)))
    cum = jnp.dot(tri_ref[...], pieces, preferred_element_type=F32)

    fox = proj(7 * BRANCH, 10 * BRANCH)
    fq = plain_pairs(fox[:, 0:BRANCH])
    fk = plain_pairs(fox[:, BRANCH:2 * BRANCH])
    fv = plain_pairs(fox[:, 2 * BRANCH:3 * BRANCH])
    one_q = jnp.where((lane >= HEAD_DIM + 3) & (lane < HEAD_DIM + 6), 1.0, 0.0)
    one_k = jnp.where((lane >= HEAD_DIM) & (lane < HEAD_DIM + 3), 1.0, 0.0)
    for h in range(N_HEADS):
        mine = (lane < 3 * N_HEADS) & ((lane & (N_HEADS - 1)) == h)
        c_h = jnp.sum(jnp.where(mine, cum, 0.0), axis=-1, keepdims=True) + ccarry[h:h + 1, 0:1]
        ccarry[h:h + 1, :] = jnp.broadcast_to(c_h[ts - 1:ts, :], (1, LANES))
        c_hi, c_mid, c_lo = (p.astype(F32) for p in _split3(c_h * LOG2E))
        cq = jnp.where(lane == HEAD_DIM, c_hi, jnp.where(lane == HEAD_DIM + 1, c_mid,
                                                          jnp.where(lane == HEAD_DIM + 2, c_lo, one_q)))
        ck = jnp.where(lane == HEAD_DIM + 3, -c_hi, jnp.where(lane == HEAD_DIM + 4, -c_mid,
                                                               jnp.where(lane == HEAD_DIM + 5, -c_lo, one_k)))
        qf_ref[0, h] = jnp.where(first, head(fq, h) * QSCALE, cq).astype(BF16)
        kf_ref[0, h] = jnp.where(first, head(fk, h), ck).astype(BF16)
        vf_ref[0, h] = jnp.where(first, head(fv, h), ones_col).astype(BF16)


def _inproj(x, ada, w_main, b_main, w_poolbd, pool_scale, conv_w, rope, tri, ts):
    b, s, d = x.shape
    n = w_main.shape[1]
    tok = lambda width: pl.BlockSpec((1, ts, width), lambda i, j: (i, j, 0))
    heads = pl.BlockSpec((1, N_HEADS, ts, LANES), lambda i, j: (i, 0, j, 0))
    heads_shape = jax.ShapeDtypeStruct((b, N_HEADS, s, LANES), BF16)
    branch_shape = jax.ShapeDtypeStruct((b, s, BRANCH), BF16)
    return pl.pallas_call(
        functools.partial(_inproj_kernel, ts=ts), grid=(b, s // ts),
        in_specs=[tok(d), pl.BlockSpec((1, 6, d), lambda i, j: (i, 0, 0)),
                  _const_spec((d, n)), _const_spec((1, n)),
                  _const_spec((BRANCH, BRANCH)), _const_spec((1, BRANCH)), _const_spec(conv_w.shape),
                  tok(LANES), tok(LANES), tok(LANES), _const_spec((ts, ts))],
        out_specs=[tok(d), tok(BRANCH), tok(BRANCH)] + [heads] * 6,
        out_shape=[jax.ShapeDtypeStruct((b, s, d), BF16), branch_shape, branch_shape] + [heads_shape] * 6,
        scratch_shapes=[pltpu.VMEM((ts + POOL_HALO, BRANCH), F32),
                        pltpu.VMEM((ts + CONV_HALO, BRANCH), F32),
                        pltpu.VMEM((8, LANES), F32)],
        compiler_params=_params(2), name="inproj_mixers",
    )(x, ada, w_main, b_main, w_poolbd, pool_scale, conv_w, *rope, tri)


def _attn_kernel(q_ref, k_ref, v_ref, o_ref, m_scr, acc_scr, *rest, tq, moba):
    i = pl.program_id(1)
    seq = k_ref.shape[2]
    n_blk = seq // MOBA_BLOCK
    nb_pad = -(-n_blk // 8) * 8

    if moba:
        kbar_scr, kb_scr = rest

        @pl.when(i == 0)
        def _():
            kbar_scr[...] = jnp.zeros_like(kbar_scr)
            lane1 = lax.broadcasted_iota(jnp.int32, (1, LANES), 1)
            for h in range(N_HEADS):
                def block_mean(n, carry):
                    start = pl.multiple_of(n * MOBA_BLOCK, MOBA_BLOCK)
                    kb = k_ref[0, h, pl.ds(start, MOBA_BLOCK), :].astype(F32)
                    mean = jnp.sum(kb, axis=0, keepdims=True) * (1.0 / MOBA_BLOCK)
                    kbar_scr[h, pl.ds(HEAD_DIM + n, 1), :] = jnp.where(lane1 < HEAD_DIM, mean, 0.0)
                    return carry
                lax.fori_loop(0, n_blk, block_mean, 0)
                for piece, val in enumerate(_split3(kbar_scr[h])):
                    kb_scr[h, piece] = val

    row = lax.broadcasted_iota(jnp.int32, (tq, tq), 0)
    col = lax.broadcasted_iota(jnp.int32, (tq, tq), 1)
    causal = col <= row

    for h in range(N_HEADS):
        q = q_ref[0, h]
        if moba:
            sc = None
            for piece in range(3):
                part = lax.dot_general(kb_scr[h, piece], q, _NT, preferred_element_type=F32)
                sc = part if sc is None else sc + part
            sc = sc[HEAD_DIM:HEAD_DIM + nb_pad, :]
            n_id = lax.broadcasted_iota(jnp.int32, sc.shape, 0).astype(F32)
            own = ((i * tq + lax.broadcasted_iota(jnp.int32, sc.shape, 1)) // MOBA_BLOCK).astype(F32)
            sc = jnp.where(n_id < own, sc, NEG)
            keep = n_id == own
            for _ in range(MOBA_TOPK):
                best = jnp.max(sc, axis=0, keepdims=True)
                cand = (sc == best) & (sc > NEG)
                pick = n_id == jnp.min(jnp.where(cand, n_id, float(LANES)), axis=0, keepdims=True)
                keep = keep | pick
                sc = jnp.where(pick, NEG, sc)
            bias_t = jnp.where(keep, 0.0, MASK)
            rows = [jnp.zeros((HEAD_DIM, tq), F32), bias_t]
            if nb_pad < LANES - HEAD_DIM:
                rows.append(jnp.zeros((LANES - HEAD_DIM - nb_pad, tq), F32))
            bias_t = jnp.concatenate(rows, axis=0)
            q = q + bias_t.T.astype(BF16)

        m_scr[...] = jnp.full_like(m_scr, MASK)
        acc_scr[...] = jnp.zeros_like(acc_scr)

        def tile(j, diag, q=q, h=h):
            start = pl.multiple_of(j * tq, tq)
            k = k_ref[0, h, pl.ds(start, tq), :]
            v = v_ref[0, h, pl.ds(start, tq), :]
            s = lax.dot_general(q, k, _NT, preferred_element_type=F32)
            if diag:
                s = jnp.where(causal, s, MASK)
            m_old = m_scr[...]
            m_new = jnp.maximum(m_old, jnp.max(s, axis=-1, keepdims=True))
            p = jnp.exp2(s - m_new)
            acc_scr[...] = acc_scr[...] * jnp.exp2(m_old - m_new) + jnp.dot(
                p.astype(BF16), v, preferred_element_type=F32)
            m_scr[...] = m_new

        def past(j, carry):
            tile(j, False)
            return carry

        lax.fori_loop(0, i, past, 0)
        tile(i, True)
        acc = acc_scr[...]
        out = acc[:, 0:HEAD_DIM] / acc[:, HEAD_DIM:HEAD_DIM + 1]
        o_ref[0, :, h * HEAD_DIM:(h + 1) * HEAD_DIM] = out.astype(BF16)


def _attention(q, k, v, tq, moba):
    b, nh, s, _ = q.shape
    whole = pl.BlockSpec((1, nh, s, LANES), lambda i, j: (i, 0, 0, 0), pipeline_mode=pl.Buffered(1))
    scratch = [pltpu.VMEM((tq, 1), F32), pltpu.VMEM((tq, LANES), F32)]
    if moba:
        scratch += [pltpu.VMEM((nh, LANES, LANES), F32), pltpu.VMEM((nh, 3, LANES, LANES), BF16)]
    return pl.pallas_call(
        functools.partial(_attn_kernel, tq=tq, moba=moba), grid=(b, s // tq),
        in_specs=[pl.BlockSpec((1, nh, tq, LANES), lambda i, j: (i, 0, j, 0)), whole, whole],
        out_specs=pl.BlockSpec((1, tq, BRANCH), lambda i, j: (i, j, 0)),
        out_shape=jax.ShapeDtypeStruct((b, s, BRANCH), BF16),
        scratch_shapes=scratch,
        compiler_params=_params(2), name="moba_attn" if moba else "fox_attn",
    )(q, k, v)


def _merge_kernel(x_ref, ada_ref, u_ref, yp_ref, yc_ref, ym_ref, yf_ref,
                  wg_ref, bg_ref, wbr_ref, wo_ref, lng_ref, lnb_ref, o_ref, *, alpha):
    d = x_ref.shape[-1]
    ub = u_ref[0]
    merged = None
    for n, y_ref in enumerate((yp_ref, yc_ref, ym_ref, yf_ref)):
        gate = jax.nn.sigmoid(jnp.dot(ub, wg_ref[:, n * d:(n + 1) * d], preferred_element_type=F32)
                              + bg_ref[:, n * d:(n + 1) * d])
        term = gate * jnp.dot(y_ref[0], wbr_ref[n], preferred_element_type=F32)
        merged = term if merged is None else merged + term
    mixed = jnp.dot(merged.astype(BF16), wo_ref[...], preferred_element_type=F32)
    res = alpha * x_ref[0] + ada_ref[0, 2:3, :] * mixed
    o_ref[0] = _layer_norm(res) * lng_ref[...] + lnb_ref[...]


def _merge(x, ada, u, ys, w_gate, b_gate, w_branch, w_o, ln_g, ln_b, ts, alpha):
    b, s, d = x.shape
    tok = lambda width: pl.BlockSpec((1, ts, width), lambda i, j: (i, j, 0))
    return pl.pallas_call(
        functools.partial(_merge_kernel, alpha=alpha), grid=(b, s // ts),
        in_specs=[tok(d), pl.BlockSpec((1, 6, d), lambda i, j: (i, 0, 0)), tok(d)] + [tok(BRANCH)] * 4
                 + [_const_spec(w_gate.shape), _const_spec(b_gate.shape), _const_spec(w_branch.shape),
                    _const_spec(w_o.shape), _const_spec((1, d)), _const_spec((1, d))],
        out_specs=tok(d), out_shape=jax.ShapeDtypeStruct((b, s, d), F32),
        compiler_params=_params(2), name="merge_out",
    )(x, ada, u, *ys, w_gate, b_gate, w_branch, w_o, ln_g, ln_b)


def _mlp_kernel(x_ref, ada_ref, wup_ref, bup_ref, wdn_ref, lng_ref, lnb_ref, o_ref, *, alpha, chunk):
    x = x_ref[0]
    u2 = (_layer_norm(x) * (1.0 + ada_ref[0, 4:5, :]) + ada_ref[0, 3:4, :]).astype(BF16)
    y = None
    for lo in range(0, wup_ref.shape[1], chunk):
        hid = jnp.dot(u2, wup_ref[:, lo:lo + chunk], preferred_element_type=F32) + bup_ref[:, lo:lo + chunk]
        hid = jnp.square(jnp.maximum(hid, 0.0)).astype(BF16)
        part = jnp.dot(hid, wdn_ref[lo:lo + chunk, :], preferred_element_type=F32)
        y = part if y is None else y + part
    res = alpha * x + ada_ref[0, 5:6, :] * y
    o_ref[0] = _layer_norm(res) * lng_ref[...] + lnb_ref[...]


def _mlp(x, ada, w_up, b_up, w_down, ln_g, ln_b, ts, alpha, chunk=1024):
    b, s, d = x.shape
    tok = pl.BlockSpec((1, ts, d), lambda i, j: (i, j, 0))
    return pl.pallas_call(
        functools.partial(_mlp_kernel, alpha=alpha, chunk=chunk), grid=(b, s // ts),
        in_specs=[tok, pl.BlockSpec((1, 6, d), lambda i, j: (i, 0, 0)),
                  _const_spec(w_up.shape), _const_spec(b_up.shape), _const_spec(w_down.shape),
                  _const_spec((1, d)), _const_spec((1, d))],
        out_specs=tok, out_shape=jax.ShapeDtypeStruct((b, s, d), F32),
        compiler_params=_params(2), name="mlp",
    )(x, ada, w_up, b_up, w_down, ln_g, ln_b)


def _tile(s):
    for ts in (512, 256):
        if s % ts == 0:
            return ts
    raise ValueError(f"sequence length {s} must be a multiple of {MOBA_BLOCK}")


def kernel(x, c, positions, w_ada, b_ada, w_in, b_in, w_pool, pool_scale, conv_w, w_branch, w_o,
           ln1_g, ln1_b, w_up, b_up, w_down, ln2_g, ln2_b):
    depth = w_ada.shape[0]
    b, s, d = x.shape
    assert d % LANES == 0 and w_in.shape[2] == N_MAIN + N_HEADS + N_HEADS * d
    assert s // MOBA_BLOCK <= LANES - HEAD_DIM
    ts = _tile(s)
    alpha = float((2 * depth) ** 0.25)

    rope = _rope_tables(positions, ts)
    ada_all = _ada_all(c, w_ada, b_ada)
    tri = jnp.tril(jnp.ones((ts, ts), BF16))
    gate0 = N_MAIN + N_HEADS

    for l in range(depth):
        pad = LANES - 3 * N_HEADS
        w_f, b_f = w_in[l][:, N_MAIN:gate0], b_in[l][N_MAIN:gate0]
        w_main = jnp.concatenate([w_in[l][:, :N_MAIN], w_f, w_f, w_f, jnp.zeros((d, pad), F32)], axis=1)
        b_main = jnp.concatenate([b_in[l][:N_MAIN], b_f, b_f, b_f, jnp.zeros((pad,), F32)])[None, :]
        w_poolbd = jax.scipy.linalg.block_diag(*[w_pool[l, g] for g in range(w_pool.shape[1])])

        u, y_pool, y_conv, qm, km, vm, qf, kf, vf = _inproj(
            x, ada_all[l], w_main.astype(BF16), b_main, w_poolbd.astype(BF16), pool_scale[l][None, :],
            conv_w[l], rope, tri, ts)
        y_moba = _attention(qm, km, vm, ts, moba=True)
        y_fox = _attention(qf, kf, vf, ts, moba=False)
        x = _merge(x, ada_all[l], u, (y_pool, y_conv, y_moba, y_fox),
                   w_in[l][:, gate0:].astype(BF16), b_in[l][None, gate0:], w_branch[l].astype(BF16),
                   w_o[l].astype(BF16), ln1_g[l][None, :], ln1_b[l][None, :], ts, alpha)
        x = _mlp(x, ada_all[l], w_up[l].astype(BF16), b_up[l][None, :], w_down[l].astype(BF16),
                 ln2_g[l][None, :], ln2_b[l][None, :], ts, alpha)
    return x
```

```python
import functools

import jax
import jax.numpy as jnp
from jax import lax
from jax.experimental import pallas as pl
from jax.experimental.pallas import tpu as pltpu

F32 = jnp.float32
BF16 = jnp.bfloat16

LANES = 128
HEAD_DIM = 64
N_HEADS = 4
BRANCH = N_HEADS * HEAD_DIM
POOL_WINDOWS = (2, 4, 8, 16)
POOL_HALO = 16
CONV_HALO = 8
MOBA_BLOCK = 256
MOBA_TOPK = 3
ROPE_THETA = 500000.0
ROPE_DIM = HEAD_DIM // 4
LN_EPS = 1e-5
LOG2E = 1.4426950408889634
QSCALE = HEAD_DIM ** -0.5 * LOG2E
MASK = -1e30
NEG = -3e38
N_MAIN = 10 * BRANCH
VMEM_LIMIT = 48 * 1024 * 1024

_NT = (((1,), (1,)), ((), ()))


def _params(n_axes):
    return pltpu.CompilerParams(dimension_semantics=("arbitrary",) * n_axes,
                                vmem_limit_bytes=VMEM_LIMIT)


def _const_spec(shape):
    nd = len(shape)
    return pl.BlockSpec(shape, lambda *_: (0,) * nd, pipeline_mode=pl.Buffered(1))


def _layer_spec(stacked, layer):
    nd = stacked.ndim - 1
    return pl.BlockSpec((None,) + stacked.shape[1:], lambda *_: (layer,) + (0,) * nd,
                        pipeline_mode=pl.Buffered(1))


def _ada_spec(ada_all, layer):
    return pl.BlockSpec((None, 1) + ada_all.shape[2:], lambda i, j: (layer, i, 0, 0))


def _layer_norm(x):
    mu = jnp.mean(x, axis=-1, keepdims=True)
    xc = x - mu
    var = jnp.mean(xc * xc, axis=-1, keepdims=True)
    return xc * lax.rsqrt(var + LN_EPS)


def _split3(x):
    hi = x.astype(BF16)
    r1 = x - hi.astype(F32)
    mid = r1.astype(BF16)
    lo = (r1 - mid.astype(F32)).astype(BF16)
    return hi, mid, lo


def _rope_kernel(pos_ref, inv_ref, c_ref, s1_ref, s2_ref):
    ang = pos_ref[0].astype(F32) * inv_ref[...]
    cos, sin = jnp.cos(ang), jnp.sin(ang)
    d = lax.broadcasted_iota(jnp.int32, ang.shape, 1) & (HEAD_DIM - 1)
    half = ROPE_DIM // 2
    c_ref[0] = jnp.where(d < ROPE_DIM, cos, 1.0)
    s1_ref[0] = jnp.where(d < half, -sin, 0.0)
    s2_ref[0] = jnp.where((d >= half) & (d < ROPE_DIM), sin, 0.0)


def _rope_tables(positions, ts):
    b, s = positions.shape
    half = ROPE_DIM // 2
    inv = ROPE_THETA ** (-jnp.arange(0, ROPE_DIM, 2, dtype=F32) / ROPE_DIM)
    inv_row = jnp.tile(inv, LANES // half)[None, :]
    tab = jax.ShapeDtypeStruct((b, s, LANES), F32)
    spec = pl.BlockSpec((1, ts, LANES), lambda i, j: (i, j, 0))
    return pl.pallas_call(
        _rope_kernel, grid=(b, s // ts),
        in_specs=[pl.BlockSpec((1, ts, 1), lambda i, j: (i, j, 0)), _const_spec((1, LANES))],
        out_specs=[spec, spec, spec], out_shape=[tab, tab, tab],
        compiler_params=_params(2), name="rope_tables",
    )(positions[..., None], inv_row)


def _ada_kernel(c_ref, w_ref, b_ref, o_ref):
    c = c_ref[...]
    act = c * jax.nn.sigmoid(c)
    o_ref[...] = jnp.dot(act, w_ref[...], preferred_element_type=F32) + b_ref[...]


def _ada_all(c, w_ada, b_ada, tn=2048):
    depth, d, n = w_ada.shape
    b = c.shape[0]
    rows = -(-b // 8) * 8
    c_pad = jnp.pad(c, ((0, rows - b), (0, 0)))
    out = pl.pallas_call(
        _ada_kernel, grid=(depth, n // tn),
        in_specs=[pl.BlockSpec((rows, d), lambda l, j: (0, 0)),
                  pl.BlockSpec((None, d, tn), lambda l, j: (l, 0, j)),
                  pl.BlockSpec((None, 1, tn), lambda l, j: (l, 0, j))],
        out_specs=pl.BlockSpec((None, rows, tn), lambda l, j: (l, 0, j)),
        out_shape=jax.ShapeDtypeStruct((depth, rows, n), F32),
        compiler_params=_params(2), name="ada_cond",
    )(c_pad, w_ada, b_ada[:, None, :])
    return out[:, :b].reshape(depth, b, 6, d)


def _inproj_kernel(x_ref, ada_ref, w_ref, b_ref, wf_ref, bf_ref, wpool_ref, pscale_ref, convw_ref,
                   rc_ref, rs1_ref, rs2_ref, tri_ref,
                   u_ref, ypool_ref, yconv_ref, qm_ref, km_ref, vm_ref, qf_ref, kf_ref, vf_ref,
                   pbuf, vbuf, ccarry, *, ts):
    t = pl.program_id(1)

    @pl.when(t == 0)
    def _():
        pbuf[0:POOL_HALO, :] = jnp.zeros((POOL_HALO, BRANCH), F32)
        vbuf[0:CONV_HALO, :] = jnp.zeros((CONV_HALO, BRANCH), F32)
        ccarry[...] = jnp.zeros_like(ccarry)

    u = _layer_norm(x_ref[0]) * (1.0 + ada_ref[0, 1:2, :]) + ada_ref[0, 0:1, :]
    ub = u.astype(BF16)
    u_ref[0] = ub

    def proj(lo, hi):
        return jnp.dot(ub, w_ref[:, lo:hi], preferred_element_type=F32) + b_ref[:, lo:hi]

    lane = lax.broadcasted_iota(jnp.int32, (ts, LANES), 1)
    row = lax.broadcasted_iota(jnp.int32, (ts, 1), 0)
    first = lane < HEAD_DIM

    p_in = proj(0, BRANCH)
    pbuf[POOL_HALO:POOL_HALO + ts, :] = p_in
    posp1 = (t * ts + row + 1).astype(F32)

    def window_sums(lo):
        sums, acc = {}, pbuf[:, lo:lo + LANES]
        for w in (1, 2, 4, 8):
            acc = acc + pltpu.roll(acc, w, 0)
            sums[2 * w] = acc[POOL_HALO:, :]
        return sums

    pooled = []
    for pair in range(2):
        w_a, w_b = POOL_WINDOWS[2 * pair], POOL_WINDOWS[2 * pair + 1]
        lo = pair * LANES
        sums = window_sums(lo)
        win = jnp.where(first, sums[w_a], sums[w_b])
        cnt = jnp.where(first, jnp.minimum(posp1, float(w_a)), jnp.minimum(posp1, float(w_b)))
        pooled.append(win / cnt - p_in[:, lo:lo + LANES])
    pooled = jnp.concatenate(pooled, axis=-1).astype(BF16)
    y_pool = jnp.dot(pooled, wpool_ref[...], preferred_element_type=F32) * pscale_ref[...]
    ypool_ref[0] = y_pool.astype(BF16)
    pbuf[0:POOL_HALO, :] = pbuf[ts:ts + POOL_HALO, :]

    conv = proj(BRANCH, 4 * BRANCH)
    v = conv[:, BRANCH:2 * BRANCH] * conv[:, 2 * BRANCH:3 * BRANCH]
    vbuf[CONV_HALO:CONV_HALO + ts, :] = v
    v_hist = vbuf[...]
    z = (convw_ref[2:3, :] * v
         + convw_ref[1:2, :] * pltpu.roll(v_hist, 1, 0)[CONV_HALO:, :]
         + convw_ref[0:1, :] * pltpu.roll(v_hist, 2, 0)[CONV_HALO:, :])
    yconv_ref[0] = (conv[:, 0:BRANCH] * z).astype(BF16)
    vbuf[0:CONV_HALO, :] = vbuf[ts:ts + CONV_HALO, :]

    rc, rs1, rs2 = rc_ref[0], rs1_ref[0], rs2_ref[0]
    half = ROPE_DIM // 2

    def rope_pairs(a):
        out = []
        for pair in range(2):
            ah = a[:, pair * LANES:(pair + 1) * LANES]
            out.append(ah * rc + pltpu.roll(ah, LANES - half, 1) * rs1 + pltpu.roll(ah, half, 1) * rs2)
        return out

    def plain_pairs(a):
        return [a[:, 0:LANES], a[:, LANES:2 * LANES]]

    def head(pairs, h):
        a = pairs[h // 2]
        return pltpu.roll(a, HEAD_DIM, 1) if h % 2 else a

    ones_col = jnp.where(lane == HEAD_DIM, 1.0, 0.0)

    moba = proj(4 * BRANCH, 7 * BRANCH)
    mq = rope_pairs(moba[:, 0:BRANCH])
    mk = rope_pairs(moba[:, BRANCH:2 * BRANCH])
    mv = plain_pairs(moba[:, 2 * BRANCH:3 * BRANCH])
    blk = (t * ts + row) // MOBA_BLOCK
    blk_onehot = jnp.where(lane == HEAD_DIM + blk, 1.0, 0.0)
    for h in range(N_HEADS):
        qm_ref[0, h] = jnp.where(first, head(mq, h) * QSCALE, 0.0).astype(BF16)
        km_ref[0, h] = jnp.where(first, head(mk, h), blk_onehot).astype(BF16)
        vm_ref[0, h] = jnp.where(first, head(mv, h), ones_col).astype(BF16)

    fl = jnp.dot(ub, wf_ref[...], preferred_element_type=F32) + bf_ref[...]
    log_f = jnp.minimum(fl, 0.0) - jnp.log1p(jnp.exp(-jnp.abs(fl)))
    hi, mid, lo = _split3(log_f)
    zero = jnp.zeros_like(hi)
    pieces = jnp.where(lane < N_HEADS, hi, jnp.where(lane < 2 * N_HEADS, mid,
                                                      jnp.where(lane < 3 * N_HEADS, lo, zero)))
    cum = jnp.dot(tri_ref[...], pieces, preferred_element_type=F32)

    fox = proj(7 * BRANCH, 10 * BRANCH)
    fq = plain_pairs(fox[:, 0:BRANCH])
    fk = plain_pairs(fox[:, BRANCH:2 * BRANCH])
    fv = plain_pairs(fox[:, 2 * BRANCH:3 * BRANCH])
    one_q = jnp.where((lane >= HEAD_DIM + 3) & (lane < HEAD_DIM + 6), 1.0, 0.0)
    one_k = jnp.where((lane >= HEAD_DIM) & (lane < HEAD_DIM + 3), 1.0, 0.0)
    for h in range(N_HEADS):
        mine = (lane < 3 * N_HEADS) & ((lane & (N_HEADS - 1)) == h)
        c_h = jnp.sum(jnp.where(mine, cum, 0.0), axis=-1, keepdims=True) + ccarry[h:h + 1, 0:1]
        ccarry[h:h + 1, :] = jnp.broadcast_to(c_h[ts - 1:ts, :], (1, LANES))
        c_hi, c_mid, c_lo = (p.astype(F32) for p in _split3(c_h * LOG2E))
        cq = jnp.where(lane == HEAD_DIM, c_hi, jnp.where(lane == HEAD_DIM + 1, c_mid,
                                                          jnp.where(lane == HEAD_DIM + 2, c_lo, one_q)))
        ck = jnp.where(lane == HEAD_DIM + 3, -c_hi, jnp.where(lane == HEAD_DIM + 4, -c_mid,
                                                               jnp.where(lane == HEAD_DIM + 5, -c_lo, one_k)))
        qf_ref[0, h] = jnp.where(first, head(fq, h) * QSCALE, cq).astype(BF16)
        kf_ref[0, h] = jnp.where(first, head(fk, h), ck).astype(BF16)
        vf_ref[0, h] = jnp.where(first, head(fv, h), ones_col).astype(BF16)


def _inproj(x, ada, w_main, b_main, w_fgt, b_fgt, w_poolbd, pool_scale, conv_w, rope, tri, ts, layer):
    b, s, d = x.shape
    tok = lambda width: pl.BlockSpec((1, ts, width), lambda i, j: (i, j, 0))
    heads = pl.BlockSpec((1, N_HEADS, ts, LANES), lambda i, j: (i, 0, j, 0))
    heads_shape = jax.ShapeDtypeStruct((b, N_HEADS, s, LANES), BF16)
    branch_shape = jax.ShapeDtypeStruct((b, s, BRANCH), BF16)
    return pl.pallas_call(
        functools.partial(_inproj_kernel, ts=ts), grid=(b, s // ts),
        in_specs=[tok(d), _ada_spec(ada, layer)]
                 + [_layer_spec(w, layer) for w in (w_main, b_main, w_fgt, b_fgt, w_poolbd, pool_scale, conv_w)]
                 + [tok(LANES), tok(LANES), tok(LANES), _const_spec((ts, ts))],
        out_specs=[tok(d), tok(BRANCH), tok(BRANCH)] + [heads] * 6,
        out_shape=[jax.ShapeDtypeStruct((b, s, d), BF16), branch_shape, branch_shape] + [heads_shape] * 6,
        scratch_shapes=[pltpu.VMEM((ts + POOL_HALO, BRANCH), F32),
                        pltpu.VMEM((ts + CONV_HALO, BRANCH), F32),
                        pltpu.VMEM((8, LANES), F32)],
        compiler_params=_params(2), name="inproj_mixers",
    )(x, ada, w_main, b_main, w_fgt, b_fgt, w_poolbd, pool_scale, conv_w, *rope, tri)


def _attn_kernel(q_ref, k_ref, v_ref, o_ref, m_scr, acc_scr, *rest, tq, moba):
    i = pl.program_id(1)
    seq = k_ref.shape[2]
    n_blk = seq // MOBA_BLOCK
    nb_pad = -(-n_blk // 8) * 8

    if moba:
        qp_scr, kbar_scr, kb_scr = rest

        @pl.when(i == 0)
        def _():
            kbar_scr[...] = jnp.zeros_like(kbar_scr)
            lane1 = lax.broadcasted_iota(jnp.int32, (1, LANES), 1)
            for h in range(N_HEADS):
                def block_mean(n, carry):
                    start = pl.multiple_of(n * MOBA_BLOCK, MOBA_BLOCK)
                    kb = k_ref[0, h, pl.ds(start, MOBA_BLOCK), :].astype(F32)
                    mean = jnp.sum(kb, axis=0, keepdims=True) * (1.0 / MOBA_BLOCK)
                    kbar_scr[h, pl.ds(HEAD_DIM + n, 1), :] = jnp.where(lane1 < HEAD_DIM, mean, 0.0)
                    return carry
                lax.fori_loop(0, n_blk, block_mean, 0)
                for piece, val in enumerate(_split3(kbar_scr[h])):
                    kb_scr[h, piece] = val

    row = lax.broadcasted_iota(jnp.int32, (tq, tq), 0)
    col = lax.broadcasted_iota(jnp.int32, (tq, tq), 1)
    causal = col <= row

    if moba:
        for h in range(N_HEADS):
            q = q_ref[0, h]
            sc = None
            for piece in range(3):
                part = lax.dot_general(kb_scr[h, piece], q, _NT, preferred_element_type=F32)
                sc = part if sc is None else sc + part
            sc = sc[HEAD_DIM:HEAD_DIM + nb_pad, :]
            n_id = lax.broadcasted_iota(jnp.int32, sc.shape, 0).astype(F32)
            own = ((i * tq + lax.broadcasted_iota(jnp.int32, sc.shape, 1)) // MOBA_BLOCK).astype(F32)
            sc = jnp.where(n_id < own, sc, NEG)
            keep = n_id == own
            for _ in range(MOBA_TOPK):
                best = jnp.max(sc, axis=0, keepdims=True)
                cand = (sc == best) & (sc > NEG)
                pick = n_id == jnp.min(jnp.where(cand, n_id, float(LANES)), axis=0, keepdims=True)
                keep = keep | pick
                sc = jnp.where(pick, NEG, sc)
            bias_t = jnp.where(keep, 0.0, MASK)
            rows = [jnp.zeros((HEAD_DIM, tq), F32), bias_t]
            if nb_pad < LANES - HEAD_DIM:
                rows.append(jnp.zeros((LANES - HEAD_DIM - nb_pad, tq), F32))
            bias_t = jnp.concatenate(rows, axis=0)
            qp_scr[h] = q + bias_t.T.astype(BF16)

    m_scr[...] = jnp.full_like(m_scr, MASK)
    acc_scr[...] = jnp.zeros_like(acc_scr)

    def tile(j, diag):
        start = pl.multiple_of(j * tq, tq)
        for h in range(N_HEADS):
            q = qp_scr[h] if moba else q_ref[0, h]
            k = k_ref[0, h, pl.ds(start, tq), :]
            v = v_ref[0, h, pl.ds(start, tq), :]
            s = lax.dot_general(q, k, _NT, preferred_element_type=F32)
            if diag:
                s = jnp.where(causal, s, MASK)
            m_old = m_scr[h]
            m_new = jnp.maximum(m_old, jnp.max(s, axis=-1, keepdims=True))
            p = jnp.exp2(s - jnp.tile(m_new, (1, tq // LANES)))
            acc_scr[h] = acc_scr[h] * jnp.exp2(m_old - m_new) + jnp.dot(
                p.astype(BF16), v, preferred_element_type=F32)
            m_scr[h] = m_new

    def past(j, carry):
        tile(j, False)
        return carry

    lax.fori_loop(0, i, past, 0)
    tile(i, True)
    for h in range(N_HEADS):
        acc = acc_scr[h]
        out = acc[:, 0:HEAD_DIM] / acc[:, HEAD_DIM:HEAD_DIM + 1]
        o_ref[0, :, h * HEAD_DIM:(h + 1) * HEAD_DIM] = out.astype(BF16)


def _attention(q, k, v, tq, moba):
    b, nh, s, _ = q.shape
    whole = pl.BlockSpec((1, nh, s, LANES), lambda i, j: (i, 0, 0, 0), pipeline_mode=pl.Buffered(1))
    scratch = [pltpu.VMEM((nh, tq, LANES), F32), pltpu.VMEM((nh, tq, LANES), F32)]
    if moba:
        scratch += [pltpu.VMEM((nh, tq, LANES), BF16), pltpu.VMEM((nh, LANES, LANES), F32),
                    pltpu.VMEM((nh, 3, LANES, LANES), BF16)]
    return pl.pallas_call(
        functools.partial(_attn_kernel, tq=tq, moba=moba), grid=(b, s // tq),
        in_specs=[pl.BlockSpec((1, nh, tq, LANES), lambda i, j: (i, 0, j, 0)), whole, whole],
        out_specs=pl.BlockSpec((1, tq, BRANCH), lambda i, j: (i, j, 0)),
        out_shape=jax.ShapeDtypeStruct((b, s, BRANCH), BF16),
        scratch_shapes=scratch,
        compiler_params=_params(2), name="moba_attn" if moba else "fox_attn",
    )(q, k, v)


def _merge_kernel(x_ref, ada_ref, u_ref, yp_ref, yc_ref, ym_ref, yf_ref,
                  wg_ref, bg_ref, wbr_ref, wo_ref, lng_ref, lnb_ref, o_ref, *, alpha):
    d = x_ref.shape[-1]
    ub = u_ref[0]
    merged = None
    for n, y_ref in enumerate((yp_ref, yc_ref, ym_ref, yf_ref)):
        gate = jax.nn.sigmoid(jnp.dot(ub, wg_ref[:, n * d:(n + 1) * d], preferred_element_type=F32)
                              + bg_ref[:, n * d:(n + 1) * d])
        term = gate * jnp.dot(y_ref[0], wbr_ref[n], preferred_element_type=F32)
        merged = term if merged is None else merged + term
    mixed = jnp.dot(merged.astype(BF16), wo_ref[...], preferred_element_type=F32)
    res = alpha * x_ref[0] + ada_ref[0, 2:3, :] * mixed
    o_ref[0] = _layer_norm(res) * lng_ref[...] + lnb_ref[...]


def _merge(x, ada, u, ys, w_gate, b_gate, w_branch, w_o, ln_g, ln_b, ts, alpha, layer):
    b, s, d = x.shape
    tok = lambda width: pl.BlockSpec((1, ts, width), lambda i, j: (i, j, 0))
    return pl.pallas_call(
        functools.partial(_merge_kernel, alpha=alpha), grid=(b, s // ts),
        in_specs=[tok(d), _ada_spec(ada, layer), tok(d)] + [tok(BRANCH)] * 4
                 + [_layer_spec(w, layer) for w in (w_gate, b_gate, w_branch, w_o, ln_g, ln_b)],
        out_specs=tok(d), out_shape=jax.ShapeDtypeStruct((b, s, d), F32),
        compiler_params=_params(2), name="merge_out",
    )(x, ada, u, *ys, w_gate, b_gate, w_branch, w_o, ln_g, ln_b)


def _mlp_kernel(x_ref, ada_ref, wup_ref, bup_ref, wdn_ref, lng_ref, lnb_ref, o_ref, *, alpha, chunk):
    x = x_ref[0]
    u2 = (_layer_norm(x) * (1.0 + ada_ref[0, 4:5, :]) + ada_ref[0, 3:4, :]).astype(BF16)
    y = None
    for lo in range(0, wup_ref.shape[1], chunk):
        hid = jnp.dot(u2, wup_ref[:, lo:lo + chunk], preferred_element_type=F32) + bup_ref[:, lo:lo + chunk]
        hid = jnp.square(jnp.maximum(hid, 0.0)).astype(BF16)
        part = jnp.dot(hid, wdn_ref[lo:lo + chunk, :], preferred_element_type=F32)
        y = part if y is None else y + part
    res = alpha * x + ada_ref[0, 5:6, :] * y
    o_ref[0] = _layer_norm(res) * lng_ref[...] + lnb_ref[...]


def _mlp(x, ada, w_up, b_up, w_down, ln_g, ln_b, ts, alpha, layer, chunk=1024):
    b, s, d = x.shape
    tok = pl.BlockSpec((1, ts, d), lambda i, j: (i, j, 0))
    return pl.pallas_call(
        functools.partial(_mlp_kernel, alpha=alpha, chunk=chunk), grid=(b, s // ts),
        in_specs=[tok, _ada_spec(ada, layer)]
                 + [_layer_spec(w, layer) for w in (w_up, b_up, w_down, ln_g, ln_b)],
        out_specs=tok, out_shape=jax.ShapeDtypeStruct((b, s, d), F32),
        compiler_params=_params(2), name="mlp",
    )(x, ada, w_up, b_up, w_down, ln_g, ln_b)


def _tile(s):
    for ts in (512, 256):
        if s % ts == 0:
            return ts
    raise ValueError(f"sequence length {s} must be a multiple of {MOBA_BLOCK}")


def kernel(x, c, positions, w_ada, b_ada, w_in, b_in, w_pool, pool_scale, conv_w, w_branch, w_o,
           ln1_g, ln1_b, w_up, b_up, w_down, ln2_g, ln2_b):
    depth = w_ada.shape[0]
    b, s, d = x.shape
    assert d % LANES == 0 and w_in.shape[2] == N_MAIN + N_HEADS + N_HEADS * d
    assert s // MOBA_BLOCK <= LANES - HEAD_DIM
    ts = _tile(s)
    alpha = float((2 * depth) ** 0.25)

    rope = _rope_tables(positions, ts)
    ada_all = _ada_all(c, w_ada, b_ada)
    tri = jnp.tril(jnp.ones((ts, ts), BF16))

    gate0 = N_MAIN + N_HEADS
    pad = LANES - 3 * N_HEADS
    w_f, b_f = w_in[:, :, N_MAIN:gate0], b_in[:, N_MAIN:gate0]
    w_main = w_in[:, :, :N_MAIN].astype(BF16)
    b_main = b_in[:, None, :N_MAIN]
    w_fgt = jnp.concatenate([w_f, w_f, w_f, jnp.zeros((depth, d, pad), F32)], axis=2).astype(BF16)
    b_fgt = jnp.concatenate([b_f, b_f, b_f, jnp.zeros((depth, pad), F32)], axis=1)[:, None, :]
    w_gate = w_in[:, :, gate0:].astype(BF16)
    b_gate = b_in[:, None, gate0:]
    w_poolbd = jnp.stack([jax.scipy.linalg.block_diag(*[w_pool[l, g] for g in range(w_pool.shape[1])])
                          for l in range(depth)]).astype(BF16)
    row3 = lambda a: a[:, None, :]
    w_branch_b, w_o_b, w_up_b, w_down_b = (w.astype(BF16) for w in (w_branch, w_o, w_up, w_down))

    for l in range(depth):
        u, y_pool, y_conv, qm, km, vm, qf, kf, vf = _inproj(
            x, ada_all, w_main, b_main, w_fgt, b_fgt, w_poolbd, row3(pool_scale), conv_w, rope, tri, ts, l)
        y_moba = _attention(qm, km, vm, ts, moba=True)
        y_fox = _attention(qf, kf, vf, ts, moba=False)
        x = _merge(x, ada_all, u, (y_pool, y_conv, y_moba, y_fox), w_gate, b_gate, w_branch_b, w_o_b,
                   row3(ln1_g), row3(ln1_b), ts, alpha, l)
        x = _mlp(x, ada_all, w_up_b, row3(b_up), w_down_b, row3(ln2_g), row3(ln2_b), ts, alpha, l)
    return x
```

```python
import functools

import jax
import jax.numpy as jnp
from jax import lax
from jax.experimental import pallas as pl
from jax.experimental.pallas import tpu as pltpu

F32 = jnp.float32
BF16 = jnp.bfloat16

LANES = 128
HEAD_DIM = 64
N_HEADS = 4
BRANCH = N_HEADS * HEAD_DIM
POOL_WINDOWS = (2, 4, 8, 16)
POOL_HALO = 16
CONV_HALO = 8
MOBA_BLOCK = 256
MOBA_TOPK = 3
ROPE_THETA = 500000.0
ROPE_DIM = HEAD_DIM // 4
LN_EPS = 1e-5
LOG2E = 1.4426950408889634
QSCALE = HEAD_DIM ** -0.5 * LOG2E
MASK = -1e30
NEG = -3e38
N_MAIN = 10 * BRANCH
VMEM_LIMIT = 48 * 1024 * 1024

_NT = (((1,), (1,)), ((), ()))


def _params(n_axes):
    return pltpu.CompilerParams(dimension_semantics=("arbitrary",) * n_axes,
                                vmem_limit_bytes=VMEM_LIMIT)


def _const_spec(shape):
    nd = len(shape)
    return pl.BlockSpec(shape, lambda *_: (0,) * nd, pipeline_mode=pl.Buffered(1))


def _layer_spec(stacked, layer):
    nd = stacked.ndim - 1
    return pl.BlockSpec((None,) + stacked.shape[1:], lambda *_: (layer,) + (0,) * nd,
                        pipeline_mode=pl.Buffered(1))


def _ada_spec(ada_all, layer):
    return pl.BlockSpec((None, 1) + ada_all.shape[2:], lambda i, j: (layer, i, 0, 0))


def _layer_norm(x):
    mu = jnp.mean(x, axis=-1, keepdims=True)
    xc = x - mu
    var = jnp.mean(xc * xc, axis=-1, keepdims=True)
    return xc * lax.rsqrt(var + LN_EPS)


def _split3(x):
    hi = x.astype(BF16)
    r1 = x - hi.astype(F32)
    mid = r1.astype(BF16)
    lo = (r1 - mid.astype(F32)).astype(BF16)
    return hi, mid, lo


def _rope_kernel(pos_ref, inv_ref, c_ref, s1_ref, s2_ref):
    ang = pos_ref[0].astype(F32) * inv_ref[...]
    cos, sin = jnp.cos(ang), jnp.sin(ang)
    d = lax.broadcasted_iota(jnp.int32, ang.shape, 1) & (HEAD_DIM - 1)
    half = ROPE_DIM // 2
    c_ref[0] = jnp.where(d < ROPE_DIM, cos, 1.0)
    s1_ref[0] = jnp.where(d < half, -sin, 0.0)
    s2_ref[0] = jnp.where((d >= half) & (d < ROPE_DIM), sin, 0.0)


def _rope_tables(positions, ts):
    b, s = positions.shape
    half = ROPE_DIM // 2
    inv = ROPE_THETA ** (-jnp.arange(0, ROPE_DIM, 2, dtype=F32) / ROPE_DIM)
    inv_row = jnp.tile(inv, LANES // half)[None, :]
    tab = jax.ShapeDtypeStruct((b, s, LANES), F32)
    spec = pl.BlockSpec((1, ts, LANES), lambda i, j: (i, j, 0))
    return pl.pallas_call(
        _rope_kernel, grid=(b, s // ts),
        in_specs=[pl.BlockSpec((1, ts, 1), lambda i, j: (i, j, 0)), _const_spec((1, LANES))],
        out_specs=[spec, spec, spec], out_shape=[tab, tab, tab],
        compiler_params=_params(2), name="rope_tables",
    )(positions[..., None], inv_row)


def _ada_kernel(c_ref, w_ref, b_ref, o_ref):
    c = c_ref[...]
    act = c * jax.nn.sigmoid(c)
    o_ref[...] = jnp.dot(act, w_ref[...], preferred_element_type=F32) + b_ref[...]


def _ada_all(c, w_ada, b_ada, tn=2048):
    depth, d, n = w_ada.shape
    b = c.shape[0]
    rows = -(-b // 8) * 8
    c_pad = jnp.pad(c, ((0, rows - b), (0, 0)))
    out = pl.pallas_call(
        _ada_kernel, grid=(depth, n // tn),
        in_specs=[pl.BlockSpec((rows, d), lambda l, j: (0, 0)),
                  pl.BlockSpec((None, d, tn), lambda l, j: (l, 0, j)),
                  pl.BlockSpec((None, 1, tn), lambda l, j: (l, 0, j))],
        out_specs=pl.BlockSpec((None, rows, tn), lambda l, j: (l, 0, j)),
        out_shape=jax.ShapeDtypeStruct((depth, rows, n), F32),
        compiler_params=_params(2), name="ada_cond",
    )(c_pad, w_ada, b_ada[:, None, :])
    return out[:, :b].reshape(depth, b, 6, d)


def _inproj_kernel(x_ref, ada_ref, w_ref, b_ref, wf_ref, bf_ref, wpool_ref, pscale_ref, convw_ref,
                   rc_ref, rs1_ref, rs2_ref, tri_ref,
                   u_ref, ypool_ref, yconv_ref, qm_ref, km_ref, vm_ref, qf_ref, kf_ref, vf_ref,
                   pbuf, vbuf, ccarry, *, ts):
    t = pl.program_id(1)

    @pl.when(t == 0)
    def _():
        pbuf[0:POOL_HALO, :] = jnp.zeros((POOL_HALO, BRANCH), F32)
        vbuf[0:CONV_HALO, :] = jnp.zeros((CONV_HALO, BRANCH), F32)
        ccarry[...] = jnp.zeros_like(ccarry)

    u = _layer_norm(x_ref[0]) * (1.0 + ada_ref[0, 1:2, :]) + ada_ref[0, 0:1, :]
    ub = u.astype(BF16)
    u_ref[0] = ub

    def proj(lo, hi):
        return jnp.dot(ub, w_ref[:, lo:hi], preferred_element_type=F32) + b_ref[:, lo:hi]

    lane = lax.broadcasted_iota(jnp.int32, (ts, LANES), 1)
    row = lax.broadcasted_iota(jnp.int32, (ts, 1), 0)
    first = lane < HEAD_DIM

    p_in = proj(0, BRANCH)
    pbuf[POOL_HALO:POOL_HALO + ts, :] = p_in
    posp1 = (t * ts + row + 1).astype(F32)

    def window_sums(lo):
        sums, acc = {}, pbuf[:, lo:lo + LANES]
        for w in (1, 2, 4, 8):
            acc = acc + pltpu.roll(acc, w, 0)
            sums[2 * w] = acc[POOL_HALO:, :]
        return sums

    pooled = []
    for pair in range(2):
        w_a, w_b = POOL_WINDOWS[2 * pair], POOL_WINDOWS[2 * pair + 1]
        lo = pair * LANES
        sums = window_sums(lo)
        win = jnp.where(first, sums[w_a], sums[w_b])
        cnt = jnp.where(first, jnp.minimum(posp1, float(w_a)), jnp.minimum(posp1, float(w_b)))
        pooled.append(win / cnt - p_in[:, lo:lo + LANES])
    pooled = jnp.concatenate(pooled, axis=-1).astype(BF16)
    y_pool = jnp.dot(pooled, wpool_ref[...], preferred_element_type=F32) * pscale_ref[...]
    ypool_ref[0] = y_pool.astype(BF16)
    pbuf[0:POOL_HALO, :] = pbuf[ts:ts + POOL_HALO, :]

    conv = proj(BRANCH, 4 * BRANCH)
    v = conv[:, BRANCH:2 * BRANCH] * conv[:, 2 * BRANCH:3 * BRANCH]
    vbuf[CONV_HALO:CONV_HALO + ts, :] = v
    v_hist = vbuf[...]
    z = (convw_ref[2:3, :] * v
         + convw_ref[1:2, :] * pltpu.roll(v_hist, 1, 0)[CONV_HALO:, :]
         + convw_ref[0:1, :] * pltpu.roll(v_hist, 2, 0)[CONV_HALO:, :])
    yconv_ref[0] = (conv[:, 0:BRANCH] * z).astype(BF16)
    vbuf[0:CONV_HALO, :] = vbuf[ts:ts + CONV_HALO, :]

    rc, rs1, rs2 = rc_ref[0], rs1_ref[0], rs2_ref[0]
    half = ROPE_DIM // 2

    def rope_pairs(a):
        out = []
        for pair in range(2):
            ah = a[:, pair * LANES:(pair + 1) * LANES]
            out.append(ah * rc + pltpu.roll(ah, LANES - half, 1) * rs1 + pltpu.roll(ah, half, 1) * rs2)
        return out

    def plain_pairs(a):
        return [a[:, 0:LANES], a[:, LANES:2 * LANES]]

    def head(pairs, h):
        a = pairs[h // 2]
        return pltpu.roll(a, HEAD_DIM, 1) if h % 2 else a

    ones_col = jnp.where(lane == HEAD_DIM, 1.0, 0.0)

    moba = proj(4 * BRANCH, 7 * BRANCH)
    mq = rope_pairs(moba[:, 0:BRANCH])
    mk = rope_pairs(moba[:, BRANCH:2 * BRANCH])
    mv = plain_pairs(moba[:, 2 * BRANCH:3 * BRANCH])
    blk = (t * ts + row) // MOBA_BLOCK
    blk_onehot = jnp.where(lane == HEAD_DIM + blk, 1.0, 0.0)
    for h in range(N_HEADS):
        qm_ref[0, h] = jnp.where(first, head(mq, h) * QSCALE, 0.0).astype(BF16)
        km_ref[0, h] = jnp.where(first, head(mk, h), blk_onehot).astype(BF16)
        vm_ref[0, h, 0] = jnp.where(first, head(mv, h), ones_col).T.astype(BF16)

    fl = jnp.dot(ub, wf_ref[...], preferred_element_type=F32) + bf_ref[...]
    log_f = jnp.minimum(fl, 0.0) - jnp.log1p(jnp.exp(-jnp.abs(fl)))
    hi, mid, lo = _split3(log_f)
    zero = jnp.zeros_like(hi)
    pieces = jnp.where(lane < N_HEADS, hi, jnp.where(lane < 2 * N_HEADS, mid,
                                                      jnp.where(lane < 3 * N_HEADS, lo, zero)))
    cum = jnp.dot(tri_ref[...], pieces, preferred_element_type=F32)

    fox = proj(7 * BRANCH, 10 * BRANCH)
    fq = plain_pairs(fox[:, 0:BRANCH])
    fk = plain_pairs(fox[:, BRANCH:2 * BRANCH])
    fv = plain_pairs(fox[:, 2 * BRANCH:3 * BRANCH])
    one_q = jnp.where((lane >= HEAD_DIM + 3) & (lane < HEAD_DIM + 6), 1.0, 0.0)
    one_k = jnp.where((lane >= HEAD_DIM) & (lane < HEAD_DIM + 3), 1.0, 0.0)
    for h in range(N_HEADS):
        mine = (lane < 3 * N_HEADS) & ((lane & (N_HEADS - 1)) == h)
        c_h = jnp.sum(jnp.where(mine, cum, 0.0), axis=-1, keepdims=True) + ccarry[h:h + 1, 0:1]
        ccarry[h:h + 1, :] = jnp.broadcast_to(c_h[ts - 1:ts, :], (1, LANES))
        c_hi, c_mid, c_lo = (p.astype(F32) for p in _split3(c_h * LOG2E))
        cq = jnp.where(lane == HEAD_DIM, c_hi, jnp.where(lane == HEAD_DIM + 1, c_mid,
                                                          jnp.where(lane == HEAD_DIM + 2, c_lo, one_q)))
        ck = jnp.where(lane == HEAD_DIM + 3, -c_hi, jnp.where(lane == HEAD_DIM + 4, -c_mid,
                                                               jnp.where(lane == HEAD_DIM + 5, -c_lo, one_k)))
        qf_ref[0, h] = jnp.where(first, head(fq, h) * QSCALE, cq).astype(BF16)
        kf_ref[0, h] = jnp.where(first, head(fk, h), ck).astype(BF16)
        vf_ref[0, h, 0] = jnp.where(first, head(fv, h), ones_col).T.astype(BF16)


def _inproj(x, ada, w_main, b_main, w_fgt, b_fgt, w_poolbd, pool_scale, conv_w, rope, tri, ts, layer):
    b, s, d = x.shape
    tok = lambda width: pl.BlockSpec((1, ts, width), lambda i, j: (i, j, 0))
    heads = pl.BlockSpec((1, N_HEADS, ts, LANES), lambda i, j: (i, 0, j, 0))
    heads_shape = jax.ShapeDtypeStruct((b, N_HEADS, s, LANES), BF16)
    heads_t = pl.BlockSpec((1, N_HEADS, 1, LANES, ts), lambda i, j: (i, 0, j, 0, 0))
    heads_t_shape = jax.ShapeDtypeStruct((b, N_HEADS, s // ts, LANES, ts), BF16)
    branch_shape = jax.ShapeDtypeStruct((b, s, BRANCH), BF16)
    return pl.pallas_call(
        functools.partial(_inproj_kernel, ts=ts), grid=(b, s // ts),
        in_specs=[tok(d), _ada_spec(ada, layer)]
                 + [_layer_spec(w, layer) for w in (w_main, b_main, w_fgt, b_fgt, w_poolbd, pool_scale, conv_w)]
                 + [tok(LANES), tok(LANES), tok(LANES), _const_spec((ts, ts))],
        out_specs=[tok(d), tok(BRANCH), tok(BRANCH)] + [heads, heads, heads_t] * 2,
        out_shape=[jax.ShapeDtypeStruct((b, s, d), BF16), branch_shape, branch_shape]
                  + [heads_shape, heads_shape, heads_t_shape] * 2,
        scratch_shapes=[pltpu.VMEM((ts + POOL_HALO, BRANCH), F32),
                        pltpu.VMEM((ts + CONV_HALO, BRANCH), F32),
                        pltpu.VMEM((8, LANES), F32)],
        compiler_params=_params(2), name="inproj_mixers",
    )(x, ada, w_main, b_main, w_fgt, b_fgt, w_poolbd, pool_scale, conv_w, *rope, tri)


def _attn_kernel(q_ref, k_ref, vt_ref, o_ref, qt_scr, m_scr, acc_scr, sa_scr, sb_scr, *rest, tq, moba):
    i = pl.program_id(1)
    seq = k_ref.shape[2]
    n_blk = seq // MOBA_BLOCK
    nb_pad = -(-n_blk // 8) * 8

    if moba:
        kbar_scr, kb_scr = rest

        @pl.when(i == 0)
        def _():
            kbar_scr[...] = jnp.zeros_like(kbar_scr)
            lane1 = lax.broadcasted_iota(jnp.int32, (1, LANES), 1)
            for h in range(N_HEADS):
                def block_mean(n, carry):
                    start = pl.multiple_of(n * MOBA_BLOCK, MOBA_BLOCK)
                    kb = k_ref[0, h, pl.ds(start, MOBA_BLOCK), :].astype(F32)
                    mean = jnp.sum(kb, axis=0, keepdims=True) * (1.0 / MOBA_BLOCK)
                    kbar_scr[h, pl.ds(HEAD_DIM + n, 1), :] = jnp.where(lane1 < HEAD_DIM, mean, 0.0)
                    return carry
                lax.fori_loop(0, n_blk, block_mean, 0)
                for piece, val in enumerate(_split3(kbar_scr[h])):
                    kb_scr[h, piece] = val

    key = lax.broadcasted_iota(jnp.int32, (tq, tq), 0)
    qry = lax.broadcasted_iota(jnp.int32, (tq, tq), 1)
    causal = key <= qry

    for h in range(N_HEADS):
        qt = q_ref[0, h].astype(F32).T
        if moba:
            qt_b = qt.astype(BF16)
            sc = None
            for piece in range(3):
                part = jnp.dot(kb_scr[h, piece], qt_b, preferred_element_type=F32)
                sc = part if sc is None else sc + part
            sc = sc[HEAD_DIM:HEAD_DIM + nb_pad, :]
            n_id = lax.broadcasted_iota(jnp.int32, sc.shape, 0).astype(F32)
            own = ((i * tq + lax.broadcasted_iota(jnp.int32, sc.shape, 1)) // MOBA_BLOCK).astype(F32)
            sc = jnp.where(n_id < own, sc, NEG)
            keep = n_id == own
            for _ in range(MOBA_TOPK):
                best = jnp.max(sc, axis=0, keepdims=True)
                cand = (sc == best) & (sc > NEG)
                pick = n_id == jnp.min(jnp.where(cand, n_id, float(LANES)), axis=0, keepdims=True)
                keep = keep | pick
                sc = jnp.where(pick, NEG, sc)
            bias_t = jnp.where(keep, 0.0, MASK)
            rows = [jnp.zeros((HEAD_DIM, tq), F32), bias_t]
            if nb_pad < LANES - HEAD_DIM:
                rows.append(jnp.zeros((LANES - HEAD_DIM - nb_pad, tq), F32))
            qt = qt + jnp.concatenate(rows, axis=0)
        qt_scr[h] = qt.astype(BF16)

    m_scr[...] = jnp.full_like(m_scr, MASK)
    acc_scr[...] = jnp.zeros_like(acc_scr)

    def scores(j, s_scr):
        start = pl.multiple_of(j * tq, tq)
        for h in range(N_HEADS):
            s_scr[h] = jnp.dot(k_ref[0, h, pl.ds(start, tq), :], qt_scr[h],
                               preferred_element_type=F32)

    def consume(j, s_scr, diag):
        for h in range(N_HEADS):
            st = s_scr[h]
            if diag:
                st = jnp.where(causal, st, MASK)
            m_old = m_scr[h]
            m_new = jnp.maximum(m_old, jnp.max(st, axis=0, keepdims=True))
            pt = jnp.exp2(st - m_new)
            acc_scr[h] = acc_scr[h] * jnp.exp2(m_old - m_new) + jnp.dot(
                vt_ref[0, h, j], pt.astype(BF16), preferred_element_type=F32)
            m_scr[h] = m_new

    scores(0, sa_scr)

    def pair(jj, carry):
        j = 2 * jj
        scores(j + 1, sb_scr)
        consume(j, sa_scr, False)
        scores(j + 2, sa_scr)
        consume(j + 1, sb_scr, False)
        return carry

    lax.fori_loop(0, i // 2, pair, 0)

    @pl.when(i % 2 == 0)
    def _():
        consume(i, sa_scr, True)

    @pl.when(i % 2 == 1)
    def _():
        scores(i, sb_scr)
        consume(i - 1, sa_scr, False)
        consume(i, sb_scr, True)

    for h in range(N_HEADS):
        acc = acc_scr[h]
        out = (acc / acc[HEAD_DIM:HEAD_DIM + 1, :]).T
        o_ref[0, :, h * HEAD_DIM:(h + 1) * HEAD_DIM] = out[:, 0:HEAD_DIM].astype(BF16)


def _attention(q, k, vt, tq, moba):
    b, nh, s, _ = q.shape
    assert vt.shape == (b, nh, s // tq, LANES, tq)
    whole = pl.BlockSpec((1, nh, s, LANES), lambda i, j: (i, 0, 0, 0), pipeline_mode=pl.Buffered(1))
    whole_t = pl.BlockSpec((1,) + vt.shape[1:], lambda i, j: (i, 0, 0, 0, 0), pipeline_mode=pl.Buffered(1))
    scratch = [pltpu.VMEM((nh, LANES, tq), BF16), pltpu.VMEM((nh, 1, tq), F32),
               pltpu.VMEM((nh, LANES, tq), F32),
               pltpu.VMEM((nh, tq, tq), F32), pltpu.VMEM((nh, tq, tq), F32)]
    if moba:
        scratch += [pltpu.VMEM((nh, LANES, LANES), F32), pltpu.VMEM((nh, 3, LANES, LANES), BF16)]
    return pl.pallas_call(
        functools.partial(_attn_kernel, tq=tq, moba=moba), grid=(b, s // tq),
        in_specs=[pl.BlockSpec((1, nh, tq, LANES), lambda i, j: (i, 0, j, 0)), whole, whole_t],
        out_specs=pl.BlockSpec((1, tq, BRANCH), lambda i, j: (i, j, 0)),
        out_shape=jax.ShapeDtypeStruct((b, s, BRANCH), BF16),
        scratch_shapes=scratch,
        compiler_params=_params(2), name="moba_attn" if moba else "fox_attn",
    )(q, k, vt)


def _merge_kernel(x_ref, ada_ref, u_ref, yp_ref, yc_ref, ym_ref, yf_ref,
                  wg_ref, bg_ref, wbr_ref, wo_ref, lng_ref, lnb_ref, o_ref, *, alpha):
    d = x_ref.shape[-1]
    ub = u_ref[0]
    merged = None
    for n, y_ref in enumerate((yp_ref, yc_ref, ym_ref, yf_ref)):
        gate = jax.nn.sigmoid(jnp.dot(ub, wg_ref[:, n * d:(n + 1) * d], preferred_element_type=F32)
                              + bg_ref[:, n * d:(n + 1) * d])
        term = gate * jnp.dot(y_ref[0], wbr_ref[n], preferred_element_type=F32)
        merged = term if merged is None else merged + term
    mixed = jnp.dot(merged.astype(BF16), wo_ref[...], preferred_element_type=F32)
    res = alpha * x_ref[0] + ada_ref[0, 2:3, :] * mixed
    o_ref[0] = _layer_norm(res) * lng_ref[...] + lnb_ref[...]


def _merge(x, ada, u, ys, w_gate, b_gate, w_branch, w_o, ln_g, ln_b, ts, alpha, layer):
    b, s, d = x.shape
    tok = lambda width: pl.BlockSpec((1, ts, width), lambda i, j: (i, j, 0))
    return pl.pallas_call(
        functools.partial(_merge_kernel, alpha=alpha), grid=(b, s // ts),
        in_specs=[tok(d), _ada_spec(ada, layer), tok(d)] + [tok(BRANCH)] * 4
                 + [_layer_spec(w, layer) for w in (w_gate, b_gate, w_branch, w_o, ln_g, ln_b)],
        out_specs=tok(d), out_shape=jax.ShapeDtypeStruct((b, s, d), F32),
        compiler_params=_params(2), name="merge_out",
    )(x, ada, u, *ys, w_gate, b_gate, w_branch, w_o, ln_g, ln_b)


def _mlp_kernel(x_ref, ada_ref, wup_ref, bup_ref, wdn_ref, lng_ref, lnb_ref, o_ref, *, alpha, chunk):
    x = x_ref[0]
    u2 = (_layer_norm(x) * (1.0 + ada_ref[0, 4:5, :]) + ada_ref[0, 3:4, :]).astype(BF16)
    y = None
    for lo in range(0, wup_ref.shape[1], chunk):
        hid = jnp.dot(u2, wup_ref[:, lo:lo + chunk], preferred_element_type=F32) + bup_ref[:, lo:lo + chunk]
        hid = jnp.square(jnp.maximum(hid, 0.0)).astype(BF16)
        part = jnp.dot(hid, wdn_ref[lo:lo + chunk, :], preferred_element_type=F32)
        y = part if y is None else y + part
    res = alpha * x + ada_ref[0, 5:6, :] * y
    o_ref[0] = _layer_norm(res) * lng_ref[...] + lnb_ref[...]


def _mlp(x, ada, w_up, b_up, w_down, ln_g, ln_b, ts, alpha, layer, chunk=1024):
    b, s, d = x.shape
    tok = pl.BlockSpec((1, ts, d), lambda i, j: (i, j, 0))
    return pl.pallas_call(
        functools.partial(_mlp_kernel, alpha=alpha, chunk=chunk), grid=(b, s // ts),
        in_specs=[tok, _ada_spec(ada, layer)]
                 + [_layer_spec(w, layer) for w in (w_up, b_up, w_down, ln_g, ln_b)],
        out_specs=tok, out_shape=jax.ShapeDtypeStruct((b, s, d), F32),
        compiler_params=_params(2), name="mlp",
    )(x, ada, w_up, b_up, w_down, ln_g, ln_b)


def _tile(s):
    for ts in (512, 256):
        if s % ts == 0:
            return ts
    raise ValueError(f"sequence length {s} must be a multiple of {MOBA_BLOCK}")


def kernel(x, c, positions, w_ada, b_ada, w_in, b_in, w_pool, pool_scale, conv_w, w_branch, w_o,
           ln1_g, ln1_b, w_up, b_up, w_down, ln2_g, ln2_b):
    depth = w_ada.shape[0]
    b, s, d = x.shape
    assert d % LANES == 0 and w_in.shape[2] == N_MAIN + N_HEADS + N_HEADS * d
    assert s // MOBA_BLOCK <= LANES - HEAD_DIM
    ts = _tile(s)
    alpha = float((2 * depth) ** 0.25)

    rope = _rope_tables(positions, ts)
    ada_all = _ada_all(c, w_ada, b_ada)
    tri = jnp.tril(jnp.ones((ts, ts), BF16))

    gate0 = N_MAIN + N_HEADS
    pad = LANES - 3 * N_HEADS
    w_f, b_f = w_in[:, :, N_MAIN:gate0], b_in[:, N_MAIN:gate0]
    w_main = w_in[:, :, :N_MAIN].astype(BF16)
    b_main = b_in[:, None, :N_MAIN]
    w_fgt = jnp.concatenate([w_f, w_f, w_f, jnp.zeros((depth, d, pad), F32)], axis=2).astype(BF16)
    b_fgt = jnp.concatenate([b_f, b_f, b_f, jnp.zeros((depth, pad), F32)], axis=1)[:, None, :]
    w_gate = w_in[:, :, gate0:].astype(BF16)
    b_gate = b_in[:, None, gate0:]
    w_poolbd = jnp.stack([jax.scipy.linalg.block_diag(*[w_pool[l, g] for g in range(w_pool.shape[1])])
                          for l in range(depth)]).astype(BF16)
    row3 = lambda a: a[:, None, :]
    w_branch_b, w_o_b, w_up_b, w_down_b = (w.astype(BF16) for w in (w_branch, w_o, w_up, w_down))

    for l in range(depth):
        u, y_pool, y_conv, qm, km, vm, qf, kf, vf = _inproj(
            x, ada_all, w_main, b_main, w_fgt, b_fgt, w_poolbd, row3(pool_scale), conv_w, rope, tri, ts, l)
        y_moba = _attention(qm, km, vm, ts, moba=True)
        y_fox = _attention(qf, kf, vf, ts, moba=False)
        x = _merge(x, ada_all, u, (y_pool, y_conv, y_moba, y_fox), w_gate, b_gate, w_branch_b, w_o_b,
                   row3(ln1_g), row3(ln1_b), ts, alpha, l)
        x = _mlp(x, ada_all, w_up_b, row3(b_up), w_down_b, row3(ln2_g), row3(ln2_b), ts, alpha, l)
    return x
```

```python
import functools

import jax
import jax.numpy as jnp
from jax import lax
from jax.experimental import pallas as pl
from jax.experimental.pallas import tpu as pltpu

F32 = jnp.float32
BF16 = jnp.bfloat16

LANES = 128
HEAD_DIM = 64
N_HEADS = 4
BRANCH = N_HEADS * HEAD_DIM
POOL_WINDOWS = (2, 4, 8, 16)
POOL_HALO = 16
CONV_HALO = 8
MOBA_BLOCK = 256
MOBA_TOPK = 3
ROPE_THETA = 500000.0
ROPE_DIM = HEAD_DIM // 4
LN_EPS = 1e-5
LOG2E = 1.4426950408889634
QSCALE = HEAD_DIM ** -0.5 * LOG2E
MASK = -1e30
NEG = -3e38
N_MAIN = 10 * BRANCH
VMEM_LIMIT = 56 * 1024 * 1024

_TN = (((0,), (0,)), ((), ()))


def _params(n_axes):
    return pltpu.CompilerParams(dimension_semantics=("arbitrary",) * n_axes,
                                vmem_limit_bytes=VMEM_LIMIT)


def _const_spec(shape):
    nd = len(shape)
    return pl.BlockSpec(shape, lambda *_: (0,) * nd, pipeline_mode=pl.Buffered(1))


def _layer_spec(stacked, layer):
    nd = stacked.ndim - 1
    return pl.BlockSpec((None,) + stacked.shape[1:], lambda *_: (layer,) + (0,) * nd,
                        pipeline_mode=pl.Buffered(1))


def _ada_spec(ada_all, layer):
    return pl.BlockSpec((None, 1) + ada_all.shape[2:], lambda i, j: (layer, i, 0, 0))


def _layer_norm(x):
    mu = jnp.mean(x, axis=-1, keepdims=True)
    xc = x - mu
    var = jnp.mean(xc * xc, axis=-1, keepdims=True)
    return xc * lax.rsqrt(var + LN_EPS)


def _split3(x):
    hi = x.astype(BF16)
    r1 = x - hi.astype(F32)
    mid = r1.astype(BF16)
    lo = (r1 - mid.astype(F32)).astype(BF16)
    return hi, mid, lo


def _rope_kernel(pos_ref, inv_ref, c_ref, s1_ref, s2_ref):
    ang = pos_ref[0].astype(F32) * inv_ref[...]
    cos, sin = jnp.cos(ang), jnp.sin(ang)
    d = lax.broadcasted_iota(jnp.int32, ang.shape, 1) & (HEAD_DIM - 1)
    half = ROPE_DIM // 2
    c_ref[0] = jnp.where(d < ROPE_DIM, cos, 1.0)
    s1_ref[0] = jnp.where(d < half, -sin, 0.0)
    s2_ref[0] = jnp.where((d >= half) & (d < ROPE_DIM), sin, 0.0)


def _rope_tables(positions, ts):
    b, s = positions.shape
    half = ROPE_DIM // 2
    inv = ROPE_THETA ** (-jnp.arange(0, ROPE_DIM, 2, dtype=F32) / ROPE_DIM)
    inv_row = jnp.tile(inv, LANES // half)[None, :]
    tab = jax.ShapeDtypeStruct((b, s, LANES), F32)
    spec = pl.BlockSpec((1, ts, LANES), lambda i, j: (i, j, 0))
    return pl.pallas_call(
        _rope_kernel, grid=(b, s // ts),
        in_specs=[pl.BlockSpec((1, ts, 1), lambda i, j: (i, j, 0)), _const_spec((1, LANES))],
        out_specs=[spec, spec, spec], out_shape=[tab, tab, tab],
        compiler_params=_params(2), name="rope_tables",
    )(positions[..., None], inv_row)


def _ada_kernel(c_ref, w_ref, b_ref, o_ref):
    c = c_ref[...]
    act = c * jax.nn.sigmoid(c)
    o_ref[...] = jnp.dot(act, w_ref[...], preferred_element_type=F32) + b_ref[...]


def _ada_all(c, w_ada, b_ada, tn=2048):
    depth, d, n = w_ada.shape
    b = c.shape[0]
    rows = -(-b // 8) * 8
    c_pad = jnp.pad(c, ((0, rows - b), (0, 0)))
    out = pl.pallas_call(
        _ada_kernel, grid=(depth, n // tn),
        in_specs=[pl.BlockSpec((rows, d), lambda l, j: (0, 0)),
                  pl.BlockSpec((None, d, tn), lambda l, j: (l, 0, j)),
                  pl.BlockSpec((None, 1, tn), lambda l, j: (l, 0, j))],
        out_specs=pl.BlockSpec((None, rows, tn), lambda l, j: (l, 0, j)),
        out_shape=jax.ShapeDtypeStruct((depth, rows, n), F32),
        compiler_params=_params(2), name="ada_cond",
    )(c_pad, w_ada, b_ada[:, None, :])
    return out[:, :b].reshape(depth, b, 6, d)


def _inproj_kernel(x_ref, ada_ref, w_ref, b_ref, wf_ref, bf_ref, wpool_ref, pscale_ref, convw_ref,
                   rc_ref, rs1_ref, rs2_ref, tri_ref,
                   u_ref, ypool_ref, yconv_ref, qm_ref, km_ref, vm_ref, qf_ref, kf_ref, vf_ref,
                   pbuf, vbuf, ccarry, *, ts):
    t = pl.program_id(1)

    @pl.when(t == 0)
    def _():
        pbuf[0:POOL_HALO, :] = jnp.zeros((POOL_HALO, BRANCH), F32)
        vbuf[0:CONV_HALO, :] = jnp.zeros((CONV_HALO, BRANCH), F32)
        ccarry[...] = jnp.zeros_like(ccarry)

    u = _layer_norm(x_ref[0]) * (1.0 + ada_ref[0, 1:2, :]) + ada_ref[0, 0:1, :]
    ub = u.astype(BF16)
    u_ref[0] = ub

    def proj(lo, hi):
        return jnp.dot(ub, w_ref[:, lo:hi], preferred_element_type=F32) + b_ref[:, lo:hi]

    lane = lax.broadcasted_iota(jnp.int32, (ts, LANES), 1)
    row = lax.broadcasted_iota(jnp.int32, (ts, 1), 0)
    first = lane < HEAD_DIM

    p_in = proj(0, BRANCH)
    pbuf[POOL_HALO:POOL_HALO + ts, :] = p_in
    posp1 = (t * ts + row + 1).astype(F32)

    def window_sums(lo):
        sums, acc = {}, pbuf[:, lo:lo + LANES]
        for w in (1, 2, 4, 8):
            acc = acc + pltpu.roll(acc, w, 0)
            sums[2 * w] = acc[POOL_HALO:, :]
        return sums

    pooled = []
    for pair in range(2):
        w_a, w_b = POOL_WINDOWS[2 * pair], POOL_WINDOWS[2 * pair + 1]
        lo = pair * LANES
        sums = window_sums(lo)
        win = jnp.where(first, sums[w_a], sums[w_b])
        cnt = jnp.where(first, jnp.minimum(posp1, float(w_a)), jnp.minimum(posp1, float(w_b)))
        pooled.append(win / cnt - p_in[:, lo:lo + LANES])
    pooled = jnp.concatenate(pooled, axis=-1).astype(BF16)
    y_pool = jnp.dot(pooled, wpool_ref[...], preferred_element_type=F32) * pscale_ref[...]
    ypool_ref[0] = y_pool.astype(BF16)
    pbuf[0:POOL_HALO, :] = pbuf[ts:ts + POOL_HALO, :]

    conv = proj(BRANCH, 4 * BRANCH)
    v = conv[:, BRANCH:2 * BRANCH] * conv[:, 2 * BRANCH:3 * BRANCH]
    vbuf[CONV_HALO:CONV_HALO + ts, :] = v
    v_hist = vbuf[...]
    z = (convw_ref[2:3, :] * v
         + convw_ref[1:2, :] * pltpu.roll(v_hist, 1, 0)[CONV_HALO:, :]
         + convw_ref[0:1, :] * pltpu.roll(v_hist, 2, 0)[CONV_HALO:, :])
    yconv_ref[0] = (conv[:, 0:BRANCH] * z).astype(BF16)
    vbuf[0:CONV_HALO, :] = vbuf[ts:ts + CONV_HALO, :]

    rc, rs1, rs2 = rc_ref[0], rs1_ref[0], rs2_ref[0]
    half = ROPE_DIM // 2

    def rope_pairs(a):
        out = []
        for pair in range(2):
            ah = a[:, pair * LANES:(pair + 1) * LANES]
            out.append(ah * rc + pltpu.roll(ah, LANES - half, 1) * rs1 + pltpu.roll(ah, half, 1) * rs2)
        return out

    def plain_pairs(a):
        return [a[:, 0:LANES], a[:, LANES:2 * LANES]]

    def head(pairs, h):
        a = pairs[h // 2]
        return pltpu.roll(a, HEAD_DIM, 1) if h % 2 else a

    ones_col = jnp.where(lane == HEAD_DIM, 1.0, 0.0)

    moba = proj(4 * BRANCH, 7 * BRANCH)
    mq = rope_pairs(moba[:, 0:BRANCH])
    mk = rope_pairs(moba[:, BRANCH:2 * BRANCH])
    mv = plain_pairs(moba[:, 2 * BRANCH:3 * BRANCH])
    blk = (t * ts + row) // MOBA_BLOCK
    blk_onehot = jnp.where(lane == HEAD_DIM + blk, 1.0, 0.0)
    for h in range(N_HEADS):
        qm_ref[0, h, 0] = jnp.where(first, head(mq, h) * QSCALE, 0.0).T.astype(BF16)
        km_ref[0, h] = jnp.where(first, head(mk, h), blk_onehot).astype(BF16)
        vm_ref[0, h, 0] = jnp.where(first, head(mv, h), ones_col).T.astype(BF16)

    fl = jnp.dot(ub, wf_ref[...], preferred_element_type=F32) + bf_ref[...]
    log_f = jnp.minimum(fl, 0.0) - jnp.log1p(jnp.exp(-jnp.abs(fl)))
    hi, mid, lo = _split3(log_f)
    zero = jnp.zeros_like(hi)
    pieces = jnp.where(lane < N_HEADS, hi, jnp.where(lane < 2 * N_HEADS, mid,
                                                      jnp.where(lane < 3 * N_HEADS, lo, zero)))
    cum = jnp.dot(tri_ref[...], pieces, preferred_element_type=F32)

    fox = proj(7 * BRANCH, 10 * BRANCH)
    fq = plain_pairs(fox[:, 0:BRANCH])
    fk = plain_pairs(fox[:, BRANCH:2 * BRANCH])
    fv = plain_pairs(fox[:, 2 * BRANCH:3 * BRANCH])
    one_q = jnp.where((lane >= HEAD_DIM + 3) & (lane < HEAD_DIM + 6), 1.0, 0.0)
    one_k = jnp.where((lane >= HEAD_DIM) & (lane < HEAD_DIM + 3), 1.0, 0.0)
    for h in range(N_HEADS):
        mine = (lane < 3 * N_HEADS) & ((lane & (N_HEADS - 1)) == h)
        c_h = jnp.sum(jnp.where(mine, cum, 0.0), axis=-1, keepdims=True) + ccarry[h:h + 1, 0:1]
        ccarry[h:h + 1, :] = jnp.broadcast_to(c_h[ts - 1:ts, :], (1, LANES))
        c_hi, c_mid, c_lo = (p.astype(F32) for p in _split3(c_h * LOG2E))
        cq = jnp.where(lane == HEAD_DIM, c_hi, jnp.where(lane == HEAD_DIM + 1, c_mid,
                                                          jnp.where(lane == HEAD_DIM + 2, c_lo, one_q)))
        ck = jnp.where(lane == HEAD_DIM + 3, -c_hi, jnp.where(lane == HEAD_DIM + 4, -c_mid,
                                                               jnp.where(lane == HEAD_DIM + 5, -c_lo, one_k)))
        qf_ref[0, h, 0] = jnp.where(first, head(fq, h) * QSCALE, cq).T.astype(BF16)
        kf_ref[0, h] = jnp.where(first, head(fk, h), ck).astype(BF16)
        vf_ref[0, h, 0] = jnp.where(first, head(fv, h), ones_col).T.astype(BF16)


def _inproj(x, ada, w_main, b_main, w_fgt, b_fgt, w_poolbd, pool_scale, conv_w, rope, tri, ts, layer):
    b, s, d = x.shape
    tok = lambda width: pl.BlockSpec((1, ts, width), lambda i, j: (i, j, 0))
    heads = pl.BlockSpec((1, N_HEADS, ts, LANES), lambda i, j: (i, 0, j, 0))
    heads_shape = jax.ShapeDtypeStruct((b, N_HEADS, s, LANES), BF16)
    heads_t = pl.BlockSpec((1, N_HEADS, 1, LANES, ts), lambda i, j: (i, 0, j, 0, 0))
    heads_t_shape = jax.ShapeDtypeStruct((b, N_HEADS, s // ts, LANES, ts), BF16)
    branch_shape = jax.ShapeDtypeStruct((b, s, BRANCH), BF16)
    return pl.pallas_call(
        functools.partial(_inproj_kernel, ts=ts), grid=(b, s // ts),
        in_specs=[tok(d), _ada_spec(ada, layer)]
                 + [_layer_spec(w, layer) for w in (w_main, b_main, w_fgt, b_fgt, w_poolbd, pool_scale, conv_w)]
                 + [tok(LANES), tok(LANES), tok(LANES), _const_spec((ts, ts))],
        out_specs=[tok(d), tok(BRANCH), tok(BRANCH)] + [heads_t, heads, heads_t] * 2,
        out_shape=[jax.ShapeDtypeStruct((b, s, d), BF16), branch_shape, branch_shape]
                  + [heads_t_shape, heads_shape, heads_t_shape] * 2,
        scratch_shapes=[pltpu.VMEM((ts + POOL_HALO, BRANCH), F32),
                        pltpu.VMEM((ts + CONV_HALO, BRANCH), F32),
                        pltpu.VMEM((8, LANES), F32)],
        compiler_params=_params(2), name="inproj_mixers",
    )(x, ada, w_main, b_main, w_fgt, b_fgt, w_poolbd, pool_scale, conv_w, *rope, tri)


def _attn_kernel(q_ref, k_ref, vt_ref, o_ref, qt_scr, m_scr, acc_scr, sa_scr, sb_scr, *rest, tq, moba):
    i = pl.program_id(1)
    seq = k_ref.shape[2]
    n_blk = seq // MOBA_BLOCK
    nb_pad = -(-n_blk // 8) * 8

    if moba:
        kbar_scr, kb_scr = rest

        @pl.when(i == 0)
        def _():
            kbar_scr[...] = jnp.zeros_like(kbar_scr)
            lane1 = lax.broadcasted_iota(jnp.int32, (1, LANES), 1)
            for h in range(N_HEADS):
                def block_mean(n, carry):
                    start = pl.multiple_of(n * MOBA_BLOCK, MOBA_BLOCK)
                    kb = k_ref[0, h, pl.ds(start, MOBA_BLOCK), :].astype(F32)
                    mean = jnp.sum(kb, axis=0, keepdims=True) * (1.0 / MOBA_BLOCK)
                    kbar_scr[h, pl.ds(HEAD_DIM + n, 1), :] = jnp.where(lane1 < HEAD_DIM, mean, 0.0)
                    return carry
                lax.fori_loop(0, n_blk, block_mean, 0)
                for piece, val in enumerate(_split3(kbar_scr[h])):
                    kb_scr[h, piece] = val

    key = lax.broadcasted_iota(jnp.int32, (tq, tq), 0)
    qry = lax.broadcasted_iota(jnp.int32, (tq, tq), 1)
    causal = key <= qry

    if moba:
        for h in range(N_HEADS):
            qt_b = q_ref[0, h, 0]
            sc = None
            for piece in range(3):
                part = jnp.dot(kb_scr[h, piece], qt_b, preferred_element_type=F32)
                sc = part if sc is None else sc + part
            sc = sc[HEAD_DIM:HEAD_DIM + nb_pad, :]
            n_id = lax.broadcasted_iota(jnp.int32, sc.shape, 0).astype(F32)
            own = ((i * tq + lax.broadcasted_iota(jnp.int32, sc.shape, 1)) // MOBA_BLOCK).astype(F32)
            sc = jnp.where(n_id < own, sc, NEG)
            keep = n_id == own
            for _ in range(MOBA_TOPK):
                best = jnp.max(sc, axis=0, keepdims=True)
                cand = (sc == best) & (sc > NEG)
                pick = n_id == jnp.min(jnp.where(cand, n_id, float(LANES)), axis=0, keepdims=True)
                keep = keep | pick
                sc = jnp.where(pick, NEG, sc)
            bias_t = jnp.where(keep, 0.0, MASK)
            rows = [jnp.zeros((HEAD_DIM, tq), F32), bias_t]
            if nb_pad < LANES - HEAD_DIM:
                rows.append(jnp.zeros((LANES - HEAD_DIM - nb_pad, tq), F32))
            qt_scr[h] = qt_b + jnp.concatenate(rows, axis=0).astype(BF16)

    def q_t(h):
        return qt_scr[h] if moba else q_ref[0, h, 0]

    m_scr[...] = jnp.full_like(m_scr, MASK)
    acc_scr[...] = jnp.zeros_like(acc_scr)

    def scores(j, s_scr):
        start = pl.multiple_of(j * tq, tq)
        for h in range(N_HEADS):
            s_scr[h] = jnp.dot(k_ref[0, h, pl.ds(start, tq), :], q_t(h),
                               preferred_element_type=F32)

    def consume(j, s_scr, diag):
        for h in range(N_HEADS):
            st = s_scr[h]
            if diag:
                st = jnp.where(causal, st, MASK)
            m_old = m_scr[h]
            m_new = jnp.maximum(m_old, jnp.max(st, axis=0, keepdims=True))
            pt = jnp.exp2(st - m_new)
            acc_scr[h] = acc_scr[h] * jnp.exp2(m_old - m_new) + jnp.dot(
                vt_ref[0, h, j], pt.astype(BF16), preferred_element_type=F32)
            m_scr[h] = m_new

    scores(0, sa_scr)

    def pair(jj, carry):
        j = 2 * jj
        scores(j + 1, sb_scr)
        consume(j, sa_scr, False)
        scores(j + 2, sa_scr)
        consume(j + 1, sb_scr, False)
        return carry

    lax.fori_loop(0, i // 2, pair, 0)

    @pl.when(i % 2 == 0)
    def _():
        consume(i, sa_scr, True)

    @pl.when(i % 2 == 1)
    def _():
        scores(i, sb_scr)
        consume(i - 1, sa_scr, False)
        consume(i, sb_scr, True)

    for h in range(N_HEADS):
        acc = acc_scr[h]
        out = acc[0:HEAD_DIM, :] / acc[HEAD_DIM:HEAD_DIM + 1, :]
        o_ref[0, h * HEAD_DIM:(h + 1) * HEAD_DIM, :] = out.astype(BF16)


def _attention(qt, k, vt, tq, moba):
    b, nh, s, _ = k.shape
    assert qt.shape == vt.shape == (b, nh, s // tq, LANES, tq)
    whole = pl.BlockSpec((1, nh, s, LANES), lambda i, j: (i, 0, 0, 0), pipeline_mode=pl.Buffered(1))
    whole_t = pl.BlockSpec((1,) + vt.shape[1:], lambda i, j: (i, 0, 0, 0, 0), pipeline_mode=pl.Buffered(1))
    scratch = [pltpu.VMEM((nh, LANES, tq), BF16), pltpu.VMEM((nh, 1, tq), F32),
               pltpu.VMEM((nh, LANES, tq), F32),
               pltpu.VMEM((nh, tq, tq), F32), pltpu.VMEM((nh, tq, tq), F32)]
    if moba:
        scratch += [pltpu.VMEM((nh, LANES, LANES), F32), pltpu.VMEM((nh, 3, LANES, LANES), BF16)]
    return pl.pallas_call(
        functools.partial(_attn_kernel, tq=tq, moba=moba), grid=(b, s // tq),
        in_specs=[pl.BlockSpec((1, nh, 1, LANES, tq), lambda i, j: (i, 0, j, 0, 0)), whole, whole_t],
        out_specs=pl.BlockSpec((1, BRANCH, tq), lambda i, j: (i, 0, j)),
        out_shape=jax.ShapeDtypeStruct((b, BRANCH, s), BF16),
        scratch_shapes=scratch,
        compiler_params=_params(2), name="moba_attn" if moba else "fox_attn",
    )(qt, k, vt)


def _merge_kernel(x_ref, ada_ref, u_ref, yp_ref, yc_ref, ym_ref, yf_ref,
                  wg_ref, bg_ref, wbr_ref, wo_ref, lng_ref, lnb_ref, o_ref, *, alpha):
    d = x_ref.shape[-1]
    ub = u_ref[0]
    merged = None
    for n, y_ref in enumerate((yp_ref, yc_ref, ym_ref, yf_ref)):
        gate = jax.nn.sigmoid(jnp.dot(ub, wg_ref[:, n * d:(n + 1) * d], preferred_element_type=F32)
                              + bg_ref[:, n * d:(n + 1) * d])
        if n < 2:
            branch = jnp.dot(y_ref[0], wbr_ref[n], preferred_element_type=F32)
        else:
            branch = lax.dot_general(y_ref[0], wbr_ref[n], _TN, preferred_element_type=F32)
        term = gate * branch
        merged = term if merged is None else merged + term
    mixed = jnp.dot(merged.astype(BF16), wo_ref[...], preferred_element_type=F32)
    res = alpha * x_ref[0] + ada_ref[0, 2:3, :] * mixed
    o_ref[0] = _layer_norm(res) * lng_ref[...] + lnb_ref[...]


def _merge(x, ada, u, ys, w_gate, b_gate, w_branch, w_o, ln_g, ln_b, ts, alpha, layer):
    b, s, d = x.shape
    tok = lambda width: pl.BlockSpec((1, ts, width), lambda i, j: (i, j, 0))
    return pl.pallas_call(
        functools.partial(_merge_kernel, alpha=alpha), grid=(b, s // ts),
        in_specs=[tok(d), _ada_spec(ada, layer), tok(d), tok(BRANCH), tok(BRANCH)]
                 + [pl.BlockSpec((1, BRANCH, ts), lambda i, j: (i, 0, j))] * 2
                 + [_layer_spec(w, layer) for w in (w_gate, b_gate, w_branch, w_o, ln_g, ln_b)],
        out_specs=tok(d), out_shape=jax.ShapeDtypeStruct((b, s, d), F32),
        compiler_params=_params(2), name="merge_out",
    )(x, ada, u, *ys, w_gate, b_gate, w_branch, w_o, ln_g, ln_b)


def _mlp_kernel(x_ref, ada_ref, wup_ref, bup_ref, wdn_ref, lng_ref, lnb_ref, o_ref, *, alpha, chunk):
    rows = x_ref.shape[1] // 2
    for r in range(2):
        x = x_ref[0, r * rows:(r + 1) * rows, :]
        u2 = (_layer_norm(x) * (1.0 + ada_ref[0, 4:5, :]) + ada_ref[0, 3:4, :]).astype(BF16)
        y = None
        for lo in range(0, wup_ref.shape[1], chunk):
            hid = jnp.dot(u2, wup_ref[:, lo:lo + chunk], preferred_element_type=F32) + bup_ref[:, lo:lo + chunk]
            hid = jnp.square(jnp.maximum(hid, 0.0)).astype(BF16)
            part = jnp.dot(hid, wdn_ref[lo:lo + chunk, :], preferred_element_type=F32)
            y = part if y is None else y + part
        res = alpha * x + ada_ref[0, 5:6, :] * y
        o_ref[0, r * rows:(r + 1) * rows, :] = _layer_norm(res) * lng_ref[...] + lnb_ref[...]


def _mlp(x, ada, w_up, b_up, w_down, ln_g, ln_b, ts, alpha, layer, chunk=1024):
    b, s, d = x.shape
    tok = pl.BlockSpec((1, ts, d), lambda i, j: (i, j, 0))
    return pl.pallas_call(
        functools.partial(_mlp_kernel, alpha=alpha, chunk=chunk), grid=(b, s // ts),
        in_specs=[tok, _ada_spec(ada, layer)]
                 + [_layer_spec(w, layer) for w in (w_up, b_up, w_down, ln_g, ln_b)],
        out_specs=tok, out_shape=jax.ShapeDtypeStruct((b, s, d), F32),
        compiler_params=_params(2), name="mlp",
    )(x, ada, w_up, b_up, w_down, ln_g, ln_b)


def _tile(s):
    for ts in (512, 256):
        if s % ts == 0:
            return ts
    raise ValueError(f"sequence length {s} must be a multiple of {MOBA_BLOCK}")


def kernel(x, c, positions, w_ada, b_ada, w_in, b_in, w_pool, pool_scale, conv_w, w_branch, w_o,
           ln1_g, ln1_b, w_up, b_up, w_down, ln2_g, ln2_b):
    depth = w_ada.shape[0]
    b, s, d = x.shape
    assert d % LANES == 0 and w_in.shape[2] == N_MAIN + N_HEADS + N_HEADS * d
    assert s // MOBA_BLOCK <= LANES - HEAD_DIM
    ts = _tile(s)
    alpha = float((2 * depth) ** 0.25)

    rope = _rope_tables(positions, ts)
    ada_all = _ada_all(c, w_ada, b_ada)
    tri = jnp.tril(jnp.ones((ts, ts), BF16))

    gate0 = N_MAIN + N_HEADS
    pad = LANES - 3 * N_HEADS
    w_f, b_f = w_in[:, :, N_MAIN:gate0], b_in[:, N_MAIN:gate0]
    w_main = w_in[:, :, :N_MAIN].astype(BF16)
    b_main = b_in[:, None, :N_MAIN]
    w_fgt = jnp.concatenate([w_f, w_f, w_f, jnp.zeros((depth, d, pad), F32)], axis=2).astype(BF16)
    b_fgt = jnp.concatenate([b_f, b_f, b_f, jnp.zeros((depth, pad), F32)], axis=1)[:, None, :]
    w_gate = w_in[:, :, gate0:].astype(BF16)
    b_gate = b_in[:, None, gate0:]
    w_poolbd = jnp.stack([jax.scipy.linalg.block_diag(*[w_pool[l, g] for g in range(w_pool.shape[1])])
                          for l in range(depth)]).astype(BF16)
    row3 = lambda a: a[:, None, :]
    w_branch_b, w_o_b, w_up_b, w_down_b = (w.astype(BF16) for w in (w_branch, w_o, w_up, w_down))

    for l in range(depth):
        u, y_pool, y_conv, qm, km, vm, qf, kf, vf = _inproj(
            x, ada_all, w_main, b_main, w_fgt, b_fgt, w_poolbd, row3(pool_scale), conv_w, rope, tri, ts, l)
        y_moba = _attention(qm, km, vm, ts, moba=True)
        y_fox = _attention(qf, kf, vf, ts, moba=False)
        x = _merge(x, ada_all, u, (y_pool, y_conv, y_moba, y_fox), w_gate, b_gate, w_branch_b, w_o_b,
                   row3(ln1_g), row3(ln1_b), ts, alpha, l)
        x = _mlp(x, ada_all, w_up_b, row3(b_up), w_down_b, row3(ln2_g), row3(ln2_b), ts, alpha, l)
    return x
```

```python
import functools

import jax
import jax.numpy as jnp
from jax import lax
from jax.experimental import pallas as pl
from jax.experimental.pallas import tpu as pltpu

F32 = jnp.float32
BF16 = jnp.bfloat16

LANES = 128
HEAD_DIM = 64
N_HEADS = 4
BRANCH = N_HEADS * HEAD_DIM
POOL_WINDOWS = (2, 4, 8, 16)
POOL_HALO = 16
CONV_HALO = 8
MOBA_BLOCK = 256
MOBA_TOPK = 3
ROPE_THETA = 500000.0
ROPE_DIM = HEAD_DIM // 4
LN_EPS = 1e-5
LOG2E = 1.4426950408889634
QSCALE = HEAD_DIM ** -0.5 * LOG2E
MASK = -1e30
NEG = -3e38
N_MAIN = 10 * BRANCH
VMEM_LIMIT = 56 * 1024 * 1024

_TN = (((0,), (0,)), ((), ()))


def _params(n_axes):
    return pltpu.CompilerParams(dimension_semantics=("arbitrary",) * n_axes,
                                vmem_limit_bytes=VMEM_LIMIT)


def _const_spec(shape):
    nd = len(shape)
    return pl.BlockSpec(shape, lambda *_: (0,) * nd, pipeline_mode=pl.Buffered(1))


def _layer_spec(stacked, layer):
    nd = stacked.ndim - 1
    return pl.BlockSpec((None,) + stacked.shape[1:], lambda *_: (layer,) + (0,) * nd,
                        pipeline_mode=pl.Buffered(1))


def _ada_spec(ada_all, layer):
    return pl.BlockSpec((None, 1) + ada_all.shape[2:], lambda i, j: (layer, i, 0, 0))


def _layer_norm(x):
    mu = jnp.mean(x, axis=-1, keepdims=True)
    xc = x - mu
    var = jnp.mean(xc * xc, axis=-1, keepdims=True)
    return xc * lax.rsqrt(var + LN_EPS)


def _split3(x):
    hi = x.astype(BF16)
    r1 = x - hi.astype(F32)
    mid = r1.astype(BF16)
    lo = (r1 - mid.astype(F32)).astype(BF16)
    return hi, mid, lo


def _rope_kernel(pos_ref, inv_ref, c_ref, s1_ref, s2_ref):
    ang = pos_ref[0].astype(F32) * inv_ref[...]
    cos, sin = jnp.cos(ang), jnp.sin(ang)
    d = lax.broadcasted_iota(jnp.int32, ang.shape, 1) & (HEAD_DIM - 1)
    half = ROPE_DIM // 2
    c_ref[0] = jnp.where(d < ROPE_DIM, cos, 1.0)
    s1_ref[0] = jnp.where(d < half, -sin, 0.0)
    s2_ref[0] = jnp.where((d >= half) & (d < ROPE_DIM), sin, 0.0)


def _rope_tables(positions, ts):
    b, s = positions.shape
    half = ROPE_DIM // 2
    inv = ROPE_THETA ** (-jnp.arange(0, ROPE_DIM, 2, dtype=F32) / ROPE_DIM)
    inv_row = jnp.tile(inv, LANES // half)[None, :]
    tab = jax.ShapeDtypeStruct((b, s, LANES), F32)
    spec = pl.BlockSpec((1, ts, LANES), lambda i, j: (i, j, 0))
    return pl.pallas_call(
        _rope_kernel, grid=(b, s // ts),
        in_specs=[pl.BlockSpec((1, ts, 1), lambda i, j: (i, j, 0)), _const_spec((1, LANES))],
        out_specs=[spec, spec, spec], out_shape=[tab, tab, tab],
        compiler_params=_params(2), name="rope_tables",
    )(positions[..., None], inv_row)


def _ada_kernel(c_ref, w_ref, b_ref, o_ref):
    c = c_ref[...]
    act = c * jax.nn.sigmoid(c)
    o_ref[...] = jnp.dot(act, w_ref[...], preferred_element_type=F32) + b_ref[...]


def _ada_all(c, w_ada, b_ada, tn=2048):
    depth, d, n = w_ada.shape
    b = c.shape[0]
    rows = -(-b // 8) * 8
    c_pad = jnp.pad(c, ((0, rows - b), (0, 0)))
    out = pl.pallas_call(
        _ada_kernel, grid=(depth, n // tn),
        in_specs=[pl.BlockSpec((rows, d), lambda l, j: (0, 0)),
                  pl.BlockSpec((None, d, tn), lambda l, j: (l, 0, j)),
                  pl.BlockSpec((None, 1, tn), lambda l, j: (l, 0, j))],
        out_specs=pl.BlockSpec((None, rows, tn), lambda l, j: (l, 0, j)),
        out_shape=jax.ShapeDtypeStruct((depth, rows, n), F32),
        compiler_params=_params(2), name="ada_cond",
    )(c_pad, w_ada, b_ada[:, None, :])
    return out[:, :b].reshape(depth, b, 6, d)


def _inproj_kernel(x_ref, ada_ref, w_ref, b_ref, wf_ref, bf_ref, wpool_ref, pscale_ref, convw_ref,
                   rc_ref, rs1_ref, rs2_ref, tri_ref,
                   u_ref, ypool_ref, yconv_ref, qm_ref, km_ref, vm_ref, qf_ref, kf_ref, vf_ref,
                   pbuf, vbuf, ccarry, *, ts):
    t = pl.program_id(1)

    @pl.when(t == 0)
    def _():
        pbuf[0:POOL_HALO, :] = jnp.zeros((POOL_HALO, BRANCH), F32)
        vbuf[0:CONV_HALO, :] = jnp.zeros((CONV_HALO, BRANCH), F32)
        ccarry[...] = jnp.zeros_like(ccarry)

    u = _layer_norm(x_ref[0]) * (1.0 + ada_ref[0, 1:2, :]) + ada_ref[0, 0:1, :]
    ub = u.astype(BF16)
    u_ref[0] = ub

    def proj(lo, hi):
        return jnp.dot(ub, w_ref[:, lo:hi], preferred_element_type=F32) + b_ref[:, lo:hi]

    lane = lax.broadcasted_iota(jnp.int32, (ts, LANES), 1)
    row = lax.broadcasted_iota(jnp.int32, (ts, 1), 0)
    first = lane < HEAD_DIM

    p_in = proj(0, BRANCH)
    pbuf[POOL_HALO:POOL_HALO + ts, :] = p_in
    posp1 = (t * ts + row + 1).astype(F32)

    def window_sums(lo):
        sums, acc = {}, pbuf[:, lo:lo + LANES]
        for w in (1, 2, 4, 8):
            acc = acc + pltpu.roll(acc, w, 0)
            sums[2 * w] = acc[POOL_HALO:, :]
        return sums

    pooled = []
    for pair in range(2):
        w_a, w_b = POOL_WINDOWS[2 * pair], POOL_WINDOWS[2 * pair + 1]
        lo = pair * LANES
        sums = window_sums(lo)
        win = jnp.where(first, sums[w_a], sums[w_b])
        cnt = jnp.where(first, jnp.minimum(posp1, float(w_a)), jnp.minimum(posp1, float(w_b)))
        pooled.append(win / cnt - p_in[:, lo:lo + LANES])
    pooled = jnp.concatenate(pooled, axis=-1).astype(BF16)
    y_pool = jnp.dot(pooled, wpool_ref[...], preferred_element_type=F32) * pscale_ref[...]
    ypool_ref[0] = y_pool.astype(BF16)
    pbuf[0:POOL_HALO, :] = pbuf[ts:ts + POOL_HALO, :]

    conv = proj(BRANCH, 4 * BRANCH)
    v = conv[:, BRANCH:2 * BRANCH] * conv[:, 2 * BRANCH:3 * BRANCH]
    vbuf[CONV_HALO:CONV_HALO + ts, :] = v
    v_hist = vbuf[...]
    z = (convw_ref[2:3, :] * v
         + convw_ref[1:2, :] * pltpu.roll(v_hist, 1, 0)[CONV_HALO:, :]
         + convw_ref[0:1, :] * pltpu.roll(v_hist, 2, 0)[CONV_HALO:, :])
    yconv_ref[0] = (conv[:, 0:BRANCH] * z).astype(BF16)
    vbuf[0:CONV_HALO, :] = vbuf[ts:ts + CONV_HALO, :]

    rc, rs1, rs2 = rc_ref[0], rs1_ref[0], rs2_ref[0]
    half = ROPE_DIM // 2

    def rope_pairs(a):
        out = []
        for pair in range(2):
            ah = a[:, pair * LANES:(pair + 1) * LANES]
            out.append(ah * rc + pltpu.roll(ah, LANES - half, 1) * rs1 + pltpu.roll(ah, half, 1) * rs2)
        return out

    def plain_pairs(a):
        return [a[:, 0:LANES], a[:, LANES:2 * LANES]]

    def head(pairs, h):
        a = pairs[h // 2]
        return pltpu.roll(a, HEAD_DIM, 1) if h % 2 else a

    ones_col = jnp.where(lane == HEAD_DIM, 1.0, 0.0)

    moba = proj(4 * BRANCH, 7 * BRANCH)
    mq = rope_pairs(moba[:, 0:BRANCH])
    mk = rope_pairs(moba[:, BRANCH:2 * BRANCH])
    mv = plain_pairs(moba[:, 2 * BRANCH:3 * BRANCH])
    blk = (t * ts + row) // MOBA_BLOCK
    blk_onehot = jnp.where(lane == HEAD_DIM + blk, 1.0, 0.0)
    for h in range(N_HEADS):
        qm_ref[0, h, 0] = jnp.where(first, head(mq, h) * QSCALE, 0.0).T.astype(BF16)
        km_ref[0, h] = jnp.where(first, head(mk, h), blk_onehot).astype(BF16)
        vm_ref[0, h, 0] = jnp.where(first, head(mv, h), ones_col).T.astype(BF16)

    fl = jnp.dot(ub, wf_ref[...], preferred_element_type=F32) + bf_ref[...]
    log_f = jnp.minimum(fl, 0.0) - jnp.log1p(jnp.exp(-jnp.abs(fl)))
    hi, mid, lo = _split3(log_f)
    zero = jnp.zeros_like(hi)
    pieces = jnp.where(lane < N_HEADS, hi, jnp.where(lane < 2 * N_HEADS, mid,
                                                      jnp.where(lane < 3 * N_HEADS, lo, zero)))
    cum = jnp.dot(tri_ref[...], pieces, preferred_element_type=F32)

    fox = proj(7 * BRANCH, 10 * BRANCH)
    fq = plain_pairs(fox[:, 0:BRANCH])
    fk = plain_pairs(fox[:, BRANCH:2 * BRANCH])
    fv = plain_pairs(fox[:, 2 * BRANCH:3 * BRANCH])
    one_q = jnp.where((lane >= HEAD_DIM + 3) & (lane < HEAD_DIM + 6), 1.0, 0.0)
    one_k = jnp.where((lane >= HEAD_DIM) & (lane < HEAD_DIM + 3), 1.0, 0.0)
    for h in range(N_HEADS):
        mine = (lane < 3 * N_HEADS) & ((lane & (N_HEADS - 1)) == h)
        c_h = jnp.sum(jnp.where(mine, cum, 0.0), axis=-1, keepdims=True) + ccarry[h:h + 1, 0:1]
        ccarry[h:h + 1, :] = jnp.broadcast_to(c_h[ts - 1:ts, :], (1, LANES))
        c_hi, c_mid, c_lo = (p.astype(F32) for p in _split3(c_h * LOG2E))
        cq = jnp.where(lane == HEAD_DIM, c_hi, jnp.where(lane == HEAD_DIM + 1, c_mid,
                                                          jnp.where(lane == HEAD_DIM + 2, c_lo, one_q)))
        ck = jnp.where(lane == HEAD_DIM + 3, -c_hi, jnp.where(lane == HEAD_DIM + 4, -c_mid,
                                                               jnp.where(lane == HEAD_DIM + 5, -c_lo, one_k)))
        qf_ref[0, h, 0] = jnp.where(first, head(fq, h) * QSCALE, cq).T.astype(BF16)
        kf_ref[0, h] = jnp.where(first, head(fk, h), ck).astype(BF16)
        vf_ref[0, h, 0] = jnp.where(first, head(fv, h), ones_col).T.astype(BF16)


def _inproj(x, ada, w_main, b_main, w_fgt, b_fgt, w_poolbd, pool_scale, conv_w, rope, tri, ts, layer):
    b, s, d = x.shape
    tok = lambda width: pl.BlockSpec((1, ts, width), lambda i, j: (i, j, 0))
    heads = pl.BlockSpec((1, N_HEADS, ts, LANES), lambda i, j: (i, 0, j, 0))
    heads_shape = jax.ShapeDtypeStruct((b, N_HEADS, s, LANES), BF16)
    heads_t = pl.BlockSpec((1, N_HEADS, 1, LANES, ts), lambda i, j: (i, 0, j, 0, 0))
    heads_t_shape = jax.ShapeDtypeStruct((b, N_HEADS, s // ts, LANES, ts), BF16)
    branch_shape = jax.ShapeDtypeStruct((b, s, BRANCH), BF16)
    return pl.pallas_call(
        functools.partial(_inproj_kernel, ts=ts), grid=(b, s // ts),
        in_specs=[tok(d), _ada_spec(ada, layer)]
                 + [_layer_spec(w, layer) for w in (w_main, b_main, w_fgt, b_fgt, w_poolbd, pool_scale, conv_w)]
                 + [tok(LANES), tok(LANES), tok(LANES), _const_spec((ts, ts))],
        out_specs=[tok(d), tok(BRANCH), tok(BRANCH)] + [heads_t, heads, heads_t] * 2,
        out_shape=[jax.ShapeDtypeStruct((b, s, d), BF16), branch_shape, branch_shape]
                  + [heads_t_shape, heads_shape, heads_t_shape] * 2,
        scratch_shapes=[pltpu.VMEM((ts + POOL_HALO, BRANCH), F32),
                        pltpu.VMEM((ts + CONV_HALO, BRANCH), F32),
                        pltpu.VMEM((8, LANES), F32)],
        compiler_params=_params(2), name="inproj_mixers",
    )(x, ada, w_main, b_main, w_fgt, b_fgt, w_poolbd, pool_scale, conv_w, *rope, tri)


def _attn_kernel(q_ref, k_ref, vt_ref, o_ref, qt_scr, m_scr, acc_scr, sa_scr, sb_scr, *rest, tk, moba):
    i = pl.program_id(1)
    seq = k_ref.shape[2]
    n_blk = seq // MOBA_BLOCK
    nb_pad = -(-n_blk // 8) * 8

    if moba:
        kbar_scr, kb_scr = rest

        @pl.when(i == 0)
        def _():
            kbar_scr[...] = jnp.zeros_like(kbar_scr)
            lane1 = lax.broadcasted_iota(jnp.int32, (1, LANES), 1)
            for h in range(N_HEADS):
                def block_mean(n, carry):
                    start = pl.multiple_of(n * MOBA_BLOCK, MOBA_BLOCK)
                    kb = k_ref[0, h, pl.ds(start, MOBA_BLOCK), :].astype(F32)
                    mean = jnp.sum(kb, axis=0, keepdims=True) * (1.0 / MOBA_BLOCK)
                    kbar_scr[h, pl.ds(HEAD_DIM + n, 1), :] = jnp.where(lane1 < HEAD_DIM, mean, 0.0)
                    return carry
                lax.fori_loop(0, n_blk, block_mean, 0)
                for piece, val in enumerate(_split3(kbar_scr[h])):
                    kb_scr[h, piece] = val

    tq = 2 * tk
    key = lax.broadcasted_iota(jnp.int32, (tk, tq), 0)
    qry = lax.broadcasted_iota(jnp.int32, (tk, tq), 1)
    first_diag = (qry >= tk) | (key <= qry)
    second_diag = key + tk <= qry

    for h in range(N_HEADS):
        qt_b = jnp.concatenate([q_ref[0, h, 0], q_ref[0, h, 1]], axis=1)
        if moba:
            sc = None
            for piece in range(3):
                part = jnp.dot(kb_scr[h, piece], qt_b, preferred_element_type=F32)
                sc = part if sc is None else sc + part
            sc = sc[HEAD_DIM:HEAD_DIM + nb_pad, :]
            n_id = lax.broadcasted_iota(jnp.int32, sc.shape, 0).astype(F32)
            own = ((i * tq + lax.broadcasted_iota(jnp.int32, sc.shape, 1)) // MOBA_BLOCK).astype(F32)
            sc = jnp.where(n_id < own, sc, NEG)
            keep = n_id == own
            for _ in range(MOBA_TOPK):
                best = jnp.max(sc, axis=0, keepdims=True)
                cand = (sc == best) & (sc > NEG)
                pick = n_id == jnp.min(jnp.where(cand, n_id, float(LANES)), axis=0, keepdims=True)
                keep = keep | pick
                sc = jnp.where(pick, NEG, sc)
            bias_t = jnp.where(keep, 0.0, MASK)
            rows = [jnp.zeros((HEAD_DIM, tq), F32), bias_t]
            if nb_pad < LANES - HEAD_DIM:
                rows.append(jnp.zeros((LANES - HEAD_DIM - nb_pad, tq), F32))
            qt_b = qt_b + jnp.concatenate(rows, axis=0).astype(BF16)
        qt_scr[h] = qt_b

    m_scr[...] = jnp.full_like(m_scr, MASK)
    acc_scr[...] = jnp.zeros_like(acc_scr)

    def keys(j):
        return pl.ds(pl.multiple_of(j * tk, tk), tk)

    def scores(j, s_scr):
        for h in range(N_HEADS):
            s_scr[h] = jnp.dot(k_ref[0, h, keys(j), :], qt_scr[h],
                               preferred_element_type=F32)

    def consume(j, s_scr, mask=None):
        for h in range(N_HEADS):
            st = s_scr[h]
            if mask is not None:
                st = jnp.where(mask, st, MASK)
            m_old = m_scr[h]
            m_new = jnp.maximum(m_old, jnp.max(st, axis=0, keepdims=True))
            pt = jnp.exp2(st - m_new)
            acc_scr[h] = acc_scr[h] * jnp.exp2(m_old - m_new) + jnp.dot(
                vt_ref[0, h, j], pt.astype(BF16), preferred_element_type=F32)
            m_scr[h] = m_new

    scores(0, sa_scr)

    def pair(jj, carry):
        j = 2 * jj
        scores(j + 1, sb_scr)
        consume(j, sa_scr)
        scores(j + 2, sa_scr)
        consume(j + 1, sb_scr)
        return carry

    lax.fori_loop(0, i, pair, 0)

    scores(2 * i + 1, sb_scr)
    consume(2 * i, sa_scr, mask=first_diag)
    consume(2 * i + 1, sb_scr, mask=second_diag)

    for h in range(N_HEADS):
        acc = acc_scr[h]
        out = acc[0:HEAD_DIM, :] / acc[HEAD_DIM:HEAD_DIM + 1, :]
        o_ref[0, h * HEAD_DIM:(h + 1) * HEAD_DIM, :] = out.astype(BF16)


def _attention(qt, k, vt, tk, moba):
    b, nh, s, _ = k.shape
    tq = 2 * tk
    assert qt.shape == vt.shape == (b, nh, s // tk, LANES, tk) and s % tq == 0
    whole = pl.BlockSpec((1, nh, s, LANES), lambda i, j: (i, 0, 0, 0), pipeline_mode=pl.Buffered(1))
    whole_t = pl.BlockSpec((1,) + vt.shape[1:], lambda i, j: (i, 0, 0, 0, 0), pipeline_mode=pl.Buffered(1))
    scratch = [pltpu.VMEM((nh, LANES, tq), BF16), pltpu.VMEM((nh, 1, tq), F32),
               pltpu.VMEM((nh, LANES, tq), F32),
               pltpu.VMEM((nh, tk, tq), F32), pltpu.VMEM((nh, tk, tq), F32)]
    if moba:
        scratch += [pltpu.VMEM((nh, LANES, LANES), F32), pltpu.VMEM((nh, 3, LANES, LANES), BF16)]
    return pl.pallas_call(
        functools.partial(_attn_kernel, tk=tk, moba=moba), grid=(b, s // tq),
        in_specs=[pl.BlockSpec((1, nh, 2, LANES, tk), lambda i, j: (i, 0, j, 0, 0)), whole, whole_t],
        out_specs=pl.BlockSpec((1, BRANCH, tq), lambda i, j: (i, 0, j)),
        out_shape=jax.ShapeDtypeStruct((b, BRANCH, s), BF16),
        scratch_shapes=scratch,
        compiler_params=_params(2), name="moba_attn" if moba else "fox_attn",
    )(qt, k, vt)


def _merge_kernel(x_ref, ada_ref, u_ref, yp_ref, yc_ref, ym_ref, yf_ref,
                  wg_ref, bg_ref, wbr_ref, wo_ref, lng_ref, lnb_ref, o_ref, *, alpha):
    d = x_ref.shape[-1]
    ub = u_ref[0]
    merged = None
    for n, y_ref in enumerate((yp_ref, yc_ref, ym_ref, yf_ref)):
        gate = jax.nn.sigmoid(jnp.dot(ub, wg_ref[:, n * d:(n + 1) * d], preferred_element_type=F32)
                              + bg_ref[:, n * d:(n + 1) * d])
        if n < 2:
            branch = jnp.dot(y_ref[0], wbr_ref[n], preferred_element_type=F32)
        else:
            branch = lax.dot_general(y_ref[0], wbr_ref[n], _TN, preferred_element_type=F32)
        term = gate * branch
        merged = term if merged is None else merged + term
    mixed = jnp.dot(merged.astype(BF16), wo_ref[...], preferred_element_type=F32)
    res = alpha * x_ref[0] + ada_ref[0, 2:3, :] * mixed
    o_ref[0] = _layer_norm(res) * lng_ref[...] + lnb_ref[...]


def _merge(x, ada, u, ys, w_gate, b_gate, w_branch, w_o, ln_g, ln_b, ts, alpha, layer):
    b, s, d = x.shape
    tok = lambda width: pl.BlockSpec((1, ts, width), lambda i, j: (i, j, 0))
    return pl.pallas_call(
        functools.partial(_merge_kernel, alpha=alpha), grid=(b, s // ts),
        in_specs=[tok(d), _ada_spec(ada, layer), tok(d), tok(BRANCH), tok(BRANCH)]
                 + [pl.BlockSpec((1, BRANCH, ts), lambda i, j: (i, 0, j))] * 2
                 + [_layer_spec(w, layer) for w in (w_gate, b_gate, w_branch, w_o, ln_g, ln_b)],
        out_specs=tok(d), out_shape=jax.ShapeDtypeStruct((b, s, d), F32),
        compiler_params=_params(2), name="merge_out",
    )(x, ada, u, *ys, w_gate, b_gate, w_branch, w_o, ln_g, ln_b)


def _mlp_kernel(x_ref, ada_ref, wup_ref, bup_ref, wdn_ref, lng_ref, lnb_ref, o_ref, *, alpha, chunk):
    rows = x_ref.shape[1] // 2
    for r in range(2):
        x = x_ref[0, r * rows:(r + 1) * rows, :]
        u2 = (_layer_norm(x) * (1.0 + ada_ref[0, 4:5, :]) + ada_ref[0, 3:4, :]).astype(BF16)
        y = None
        for lo in range(0, wup_ref.shape[1], chunk):
            hid = jnp.dot(u2, wup_ref[:, lo:lo + chunk], preferred_element_type=F32) + bup_ref[:, lo:lo + chunk]
            hid = jnp.square(jnp.maximum(hid, 0.0)).astype(BF16)
            part = jnp.dot(hid, wdn_ref[lo:lo + chunk, :], preferred_element_type=F32)
            y = part if y is None else y + part
        res = alpha * x + ada_ref[0, 5:6, :] * y
        o_ref[0, r * rows:(r + 1) * rows, :] = _layer_norm(res) * lng_ref[...] + lnb_ref[...]


def _mlp(x, ada, w_up, b_up, w_down, ln_g, ln_b, ts, alpha, layer, chunk=1024):
    b, s, d = x.shape
    tok = pl.BlockSpec((1, ts, d), lambda i, j: (i, j, 0))
    return pl.pallas_call(
        functools.partial(_mlp_kernel, alpha=alpha, chunk=chunk), grid=(b, s // ts),
        in_specs=[tok, _ada_spec(ada, layer)]
                 + [_layer_spec(w, layer) for w in (w_up, b_up, w_down, ln_g, ln_b)],
        out_specs=tok, out_shape=jax.ShapeDtypeStruct((b, s, d), F32),
        compiler_params=_params(2), name="mlp",
    )(x, ada, w_up, b_up, w_down, ln_g, ln_b)


def _tile(s):
    for ts in (512, 256):
        if s % (2 * ts) == 0:
            return ts
    raise ValueError(f"sequence length {s} must be a multiple of {2 * MOBA_BLOCK}")


def kernel(x, c, positions, w_ada, b_ada, w_in, b_in, w_pool, pool_scale, conv_w, w_branch, w_o,
           ln1_g, ln1_b, w_up, b_up, w_down, ln2_g, ln2_b):
    depth = w_ada.shape[0]
    b, s, d = x.shape
    assert d % LANES == 0 and w_in.shape[2] == N_MAIN + N_HEADS + N_HEADS * d
    assert s // MOBA_BLOCK <= LANES - HEAD_DIM
    ts = _tile(s)
    alpha = float((2 * depth) ** 0.25)

    rope = _rope_tables(positions, ts)
    ada_all = _ada_all(c, w_ada, b_ada)
    tri = jnp.tril(jnp.ones((ts, ts), BF16))

    gate0 = N_MAIN + N_HEADS
    pad = LANES - 3 * N_HEADS
    w_f, b_f = w_in[:, :, N_MAIN:gate0], b_in[:, N_MAIN:gate0]
    w_main = w_in[:, :, :N_MAIN].astype(BF16)
    b_main = b_in[:, None, :N_MAIN]
    w_fgt = jnp.concatenate([w_f, w_f, w_f, jnp.zeros((depth, d, pad), F32)], axis=2).astype(BF16)
    b_fgt = jnp.concatenate([b_f, b_f, b_f, jnp.zeros((depth, pad), F32)], axis=1)[:, None, :]
    w_gate = w_in[:, :, gate0:].astype(BF16)
    b_gate = b_in[:, None, gate0:]
    w_poolbd = jnp.stack([jax.scipy.linalg.block_diag(*[w_pool[l, g] for g in range(w_pool.shape[1])])
                          for l in range(depth)]).astype(BF16)
    row3 = lambda a: a[:, None, :]
    w_branch_b, w_o_b, w_up_b, w_down_b = (w.astype(BF16) for w in (w_branch, w_o, w_up, w_down))

    for l in range(depth):
        u, y_pool, y_conv, qm, km, vm, qf, kf, vf = _inproj(
            x, ada_all, w_main, b_main, w_fgt, b_fgt, w_poolbd, row3(pool_scale), conv_w, rope, tri, ts, l)
        y_moba = _attention(qm, km, vm, ts, moba=True)
        y_fox = _attention(qf, kf, vf, ts, moba=False)
        x = _merge(x, ada_all, u, (y_pool, y_conv, y_moba, y_fox), w_gate, b_gate, w_branch_b, w_o_b,
                   row3(ln1_g), row3(ln1_b), ts, alpha, l)
        x = _mlp(x, ada_all, w_up_b, row3(b_up), w_down_b, row3(ln2_g), row3(ln2_b), ts, alpha, l)
    return x
```

```python
import functools

import jax
import jax.numpy as jnp
from jax import lax
from jax.experimental import pallas as pl
from jax.experimental.pallas import tpu as pltpu

F32 = jnp.float32
BF16 = jnp.bfloat16

LANES = 128
HEAD_DIM = 64
N_HEADS = 4
BRANCH = N_HEADS * HEAD_DIM
POOL_WINDOWS = (2, 4, 8, 16)
POOL_HALO = 16
CONV_HALO = 8
MOBA_BLOCK = 256
MOBA_TOPK = 3
ROPE_THETA = 500000.0
ROPE_DIM = HEAD_DIM // 4
LN_EPS = 1e-5
LOG2E = 1.4426950408889634
QSCALE = HEAD_DIM ** -0.5 * LOG2E
MASK = -1e30
NEG = -3e38
N_MAIN = 10 * BRANCH
VMEM_LIMIT = 56 * 1024 * 1024

_TN = (((0,), (0,)), ((), ()))


def _params(n_axes):
    return pltpu.CompilerParams(dimension_semantics=("arbitrary",) * n_axes,
                                vmem_limit_bytes=VMEM_LIMIT)


def _const_spec(shape):
    nd = len(shape)
    return pl.BlockSpec(shape, lambda *_: (0,) * nd, pipeline_mode=pl.Buffered(1))


def _layer_spec(stacked, layer):
    nd = stacked.ndim - 1
    return pl.BlockSpec((None,) + stacked.shape[1:], lambda *_: (layer,) + (0,) * nd,
                        pipeline_mode=pl.Buffered(1))


def _ada_spec(ada_all, layer):
    return pl.BlockSpec((None, 1) + ada_all.shape[2:], lambda i, j: (layer, i, 0, 0))


def _layer_norm(x):
    mu = jnp.mean(x, axis=-1, keepdims=True)
    xc = x - mu
    var = jnp.mean(xc * xc, axis=-1, keepdims=True)
    return xc * lax.rsqrt(var + LN_EPS)


def _split3(x):
    hi = x.astype(BF16)
    r1 = x - hi.astype(F32)
    mid = r1.astype(BF16)
    lo = (r1 - mid.astype(F32)).astype(BF16)
    return hi, mid, lo


def _rope_kernel(pos_ref, inv_ref, c_ref, s1_ref, s2_ref):
    ang = pos_ref[0].astype(F32) * inv_ref[...]
    cos, sin = jnp.cos(ang), jnp.sin(ang)
    d = lax.broadcasted_iota(jnp.int32, ang.shape, 1) & (HEAD_DIM - 1)
    half = ROPE_DIM // 2
    c_ref[0] = jnp.where(d < ROPE_DIM, cos, 1.0)
    s1_ref[0] = jnp.where(d < half, -sin, 0.0)
    s2_ref[0] = jnp.where((d >= half) & (d < ROPE_DIM), sin, 0.0)


def _rope_tables(positions, ts):
    b, s = positions.shape
    half = ROPE_DIM // 2
    inv = ROPE_THETA ** (-jnp.arange(0, ROPE_DIM, 2, dtype=F32) / ROPE_DIM)
    inv_row = jnp.tile(inv, LANES // half)[None, :]
    tab = jax.ShapeDtypeStruct((b, s, LANES), F32)
    spec = pl.BlockSpec((1, ts, LANES), lambda i, j: (i, j, 0))
    return pl.pallas_call(
        _rope_kernel, grid=(b, s // ts),
        in_specs=[pl.BlockSpec((1, ts, 1), lambda i, j: (i, j, 0)), _const_spec((1, LANES))],
        out_specs=[spec, spec, spec], out_shape=[tab, tab, tab],
        compiler_params=_params(2), name="rope_tables",
    )(positions[..., None], inv_row)


def _ada_kernel(c_ref, w_ref, b_ref, o_ref):
    c = c_ref[...]
    act = c * jax.nn.sigmoid(c)
    o_ref[...] = jnp.dot(act, w_ref[...], preferred_element_type=F32) + b_ref[...]


def _ada_all(c, w_ada, b_ada, tn=2048):
    depth, d, n = w_ada.shape
    b = c.shape[0]
    rows = -(-b // 8) * 8
    c_pad = jnp.pad(c, ((0, rows - b), (0, 0)))
    out = pl.pallas_call(
        _ada_kernel, grid=(depth, n // tn),
        in_specs=[pl.BlockSpec((rows, d), lambda l, j: (0, 0)),
                  pl.BlockSpec((None, d, tn), lambda l, j: (l, 0, j)),
                  pl.BlockSpec((None, 1, tn), lambda l, j: (l, 0, j))],
        out_specs=pl.BlockSpec((None, rows, tn), lambda l, j: (l, 0, j)),
        out_shape=jax.ShapeDtypeStruct((depth, rows, n), F32),
        compiler_params=_params(2), name="ada_cond",
    )(c_pad, w_ada, b_ada[:, None, :])
    return out[:, :b].reshape(depth, b, 6, d)


def _inproj_kernel(x_ref, ada_ref, w_ref, b_ref, wf_ref, bf_ref, wpool_ref, pscale_ref, convw_ref,
                   rc_ref, rs1_ref, rs2_ref, tri_ref,
                   u_ref, ypool_ref, yconv_ref, qm_ref, km_ref, vm_ref, qf_ref, kf_ref, vf_ref,
                   pbuf, vbuf, ccarry, *, ts):
    t = pl.program_id(1)

    @pl.when(t == 0)
    def _():
        pbuf[0:POOL_HALO, :] = jnp.zeros((POOL_HALO, BRANCH), F32)
        vbuf[0:CONV_HALO, :] = jnp.zeros((CONV_HALO, BRANCH), F32)
        ccarry[...] = jnp.zeros_like(ccarry)

    u = _layer_norm(x_ref[0]) * (1.0 + ada_ref[0, 1:2, :]) + ada_ref[0, 0:1, :]
    ub = u.astype(BF16)
    u_ref[0] = ub

    def proj(lo, hi):
        return jnp.dot(ub, w_ref[:, lo:hi], preferred_element_type=F32) + b_ref[:, lo:hi]

    lane = lax.broadcasted_iota(jnp.int32, (ts, LANES), 1)
    row = lax.broadcasted_iota(jnp.int32, (ts, 1), 0)
    first = lane < HEAD_DIM

    p_in = proj(0, BRANCH)
    pbuf[POOL_HALO:POOL_HALO + ts, :] = p_in
    posp1 = (t * ts + row + 1).astype(F32)

    def window_sums(lo):
        sums, acc = {}, pbuf[:, lo:lo + LANES]
        for w in (1, 2, 4, 8):
            acc = acc + pltpu.roll(acc, w, 0)
            sums[2 * w] = acc[POOL_HALO:, :]
        return sums

    pooled = []
    for pair in range(2):
        w_a, w_b = POOL_WINDOWS[2 * pair], POOL_WINDOWS[2 * pair + 1]
        lo = pair * LANES
        sums = window_sums(lo)
        win = jnp.where(first, sums[w_a], sums[w_b])
        cnt = jnp.where(first, jnp.minimum(posp1, float(w_a)), jnp.minimum(posp1, float(w_b)))
        pooled.append(win / cnt - p_in[:, lo:lo + LANES])
    pooled = jnp.concatenate(pooled, axis=-1).astype(BF16)
    y_pool = jnp.dot(pooled, wpool_ref[...], preferred_element_type=F32) * pscale_ref[...]
    ypool_ref[0] = y_pool.astype(BF16)
    pbuf[0:POOL_HALO, :] = pbuf[ts:ts + POOL_HALO, :]

    conv = proj(BRANCH, 4 * BRANCH)
    v = conv[:, BRANCH:2 * BRANCH] * conv[:, 2 * BRANCH:3 * BRANCH]
    vbuf[CONV_HALO:CONV_HALO + ts, :] = v
    v_hist = vbuf[...]
    z = (convw_ref[2:3, :] * v
         + convw_ref[1:2, :] * pltpu.roll(v_hist, 1, 0)[CONV_HALO:, :]
         + convw_ref[0:1, :] * pltpu.roll(v_hist, 2, 0)[CONV_HALO:, :])
    yconv_ref[0] = (conv[:, 0:BRANCH] * z).astype(BF16)
    vbuf[0:CONV_HALO, :] = vbuf[ts:ts + CONV_HALO, :]

    rc, rs1, rs2 = rc_ref[0], rs1_ref[0], rs2_ref[0]
    half = ROPE_DIM // 2

    def rope_pairs(a):
        out = []
        for pair in range(2):
            ah = a[:, pair * LANES:(pair + 1) * LANES]
            out.append(ah * rc + pltpu.roll(ah, LANES - half, 1) * rs1 + pltpu.roll(ah, half, 1) * rs2)
        return out

    def plain_pairs(a):
        return [a[:, 0:LANES], a[:, LANES:2 * LANES]]

    def head(pairs, h):
        a = pairs[h // 2]
        return pltpu.roll(a, HEAD_DIM, 1) if h % 2 else a

    ones_col = jnp.where(lane == HEAD_DIM, 1.0, 0.0)

    moba = proj(4 * BRANCH, 7 * BRANCH)
    mq = rope_pairs(moba[:, 0:BRANCH])
    mk = rope_pairs(moba[:, BRANCH:2 * BRANCH])
    mv = plain_pairs(moba[:, 2 * BRANCH:3 * BRANCH])
    blk = (t * ts + row) // MOBA_BLOCK
    blk_onehot = jnp.where(lane == HEAD_DIM + blk, 1.0, 0.0)
    for h in range(N_HEADS):
        qm_ref[0, h, 0] = jnp.where(first, head(mq, h) * QSCALE, 0.0).T.astype(BF16)
        km_ref[0, h] = jnp.where(first, head(mk, h), blk_onehot).astype(BF16)
        vm_ref[0, h, 0] = jnp.where(first, head(mv, h), ones_col).T.astype(BF16)

    fl = jnp.dot(ub, wf_ref[...], preferred_element_type=F32) + bf_ref[...]
    log_f = jnp.minimum(fl, 0.0) - jnp.log1p(jnp.exp(-jnp.abs(fl)))
    hi, mid, lo = _split3(log_f)
    zero = jnp.zeros_like(hi)
    pieces = jnp.where(lane < N_HEADS, hi, jnp.where(lane < 2 * N_HEADS, mid,
                                                      jnp.where(lane < 3 * N_HEADS, lo, zero)))
    cum = jnp.dot(tri_ref[...], pieces, preferred_element_type=F32)

    fox = proj(7 * BRANCH, 10 * BRANCH)
    fq = plain_pairs(fox[:, 0:BRANCH])
    fk = plain_pairs(fox[:, BRANCH:2 * BRANCH])
    fv = plain_pairs(fox[:, 2 * BRANCH:3 * BRANCH])
    one_q = jnp.where((lane >= HEAD_DIM + 3) & (lane < HEAD_DIM + 6), 1.0, 0.0)
    one_k = jnp.where((lane >= HEAD_DIM) & (lane < HEAD_DIM + 3), 1.0, 0.0)
    for h in range(N_HEADS):
        mine = (lane < 3 * N_HEADS) & ((lane & (N_HEADS - 1)) == h)
        c_h = jnp.sum(jnp.where(mine, cum, 0.0), axis=-1, keepdims=True) + ccarry[h:h + 1, 0:1]
        ccarry[h:h + 1, :] = jnp.broadcast_to(c_h[ts - 1:ts, :], (1, LANES))
        c_hi, c_mid, c_lo = (p.astype(F32) for p in _split3(c_h * LOG2E))
        cq = jnp.where(lane == HEAD_DIM, c_hi, jnp.where(lane == HEAD_DIM + 1, c_mid,
                                                          jnp.where(lane == HEAD_DIM + 2, c_lo, one_q)))
        ck = jnp.where(lane == HEAD_DIM + 3, -c_hi, jnp.where(lane == HEAD_DIM + 4, -c_mid,
                                                               jnp.where(lane == HEAD_DIM + 5, -c_lo, one_k)))
        qf_ref[0, h, 0] = jnp.where(first, head(fq, h) * QSCALE, cq).T.astype(BF16)
        kf_ref[0, h] = jnp.where(first, head(fk, h), ck).astype(BF16)
        vf_ref[0, h, 0] = jnp.where(first, head(fv, h), ones_col).T.astype(BF16)


def _inproj(x, ada, w_main, b_main, w_fgt, b_fgt, w_poolbd, pool_scale, conv_w, rope, tri, ts, layer):
    b, s, d = x.shape
    tok = lambda width: pl.BlockSpec((1, ts, width), lambda i, j: (i, j, 0))
    heads = pl.BlockSpec((1, N_HEADS, ts, LANES), lambda i, j: (i, 0, j, 0))
    heads_shape = jax.ShapeDtypeStruct((b, N_HEADS, s, LANES), BF16)
    heads_t = pl.BlockSpec((1, N_HEADS, 1, LANES, ts), lambda i, j: (i, 0, j, 0, 0))
    heads_t_shape = jax.ShapeDtypeStruct((b, N_HEADS, s // ts, LANES, ts), BF16)
    branch_shape = jax.ShapeDtypeStruct((b, s, BRANCH), BF16)
    return pl.pallas_call(
        functools.partial(_inproj_kernel, ts=ts), grid=(b, s // ts),
        in_specs=[tok(d), _ada_spec(ada, layer)]
                 + [_layer_spec(w, layer) for w in (w_main, b_main, w_fgt, b_fgt, w_poolbd, pool_scale, conv_w)]
                 + [tok(LANES), tok(LANES), tok(LANES), _const_spec((ts, ts))],
        out_specs=[tok(d), tok(BRANCH), tok(BRANCH)] + [heads_t, heads, heads_t] * 2,
        out_shape=[jax.ShapeDtypeStruct((b, s, d), BF16), branch_shape, branch_shape]
                  + [heads_t_shape, heads_shape, heads_t_shape] * 2,
        scratch_shapes=[pltpu.VMEM((ts + POOL_HALO, BRANCH), F32),
                        pltpu.VMEM((ts + CONV_HALO, BRANCH), F32),
                        pltpu.VMEM((8, LANES), F32)],
        compiler_params=_params(2), name="inproj_mixers",
    )(x, ada, w_main, b_main, w_fgt, b_fgt, w_poolbd, pool_scale, conv_w, *rope, tri)


def _attn_kernel(q_ref, k_ref, vt_ref, o_ref, qt_scr, m_scr, acc_scr, sa_scr, sb_scr, *rest, tq, moba):
    i = pl.program_id(1)
    seq = k_ref.shape[2]
    n_blk = seq // MOBA_BLOCK
    nb_pad = -(-n_blk // 8) * 8

    if moba:
        kbar_scr, kb_scr = rest

        @pl.when(i == 0)
        def _():
            kbar_scr[...] = jnp.zeros_like(kbar_scr)
            lane1 = lax.broadcasted_iota(jnp.int32, (1, LANES), 1)
            for h in range(N_HEADS):
                def block_mean(n, carry):
                    start = pl.multiple_of(n * MOBA_BLOCK, MOBA_BLOCK)
                    kb = k_ref[0, h, pl.ds(start, MOBA_BLOCK), :].astype(F32)
                    mean = jnp.sum(kb, axis=0, keepdims=True) * (1.0 / MOBA_BLOCK)
                    kbar_scr[h, pl.ds(HEAD_DIM + n, 1), :] = jnp.where(lane1 < HEAD_DIM, mean, 0.0)
                    return carry
                lax.fori_loop(0, n_blk, block_mean, 0)
                for piece, val in enumerate(_split3(kbar_scr[h])):
                    kb_scr[h, piece] = val

    key = lax.broadcasted_iota(jnp.int32, (tq, tq), 0)
    qry = lax.broadcasted_iota(jnp.int32, (tq, tq), 1)
    causal = key <= qry

    if moba:
        for h in range(N_HEADS):
            qt_b = q_ref[0, h, 0]
            sc = None
            for piece in range(3):
                part = jnp.dot(kb_scr[h, piece], qt_b, preferred_element_type=F32)
                sc = part if sc is None else sc + part
            sc = sc[HEAD_DIM:HEAD_DIM + nb_pad, :]
            n_id = lax.broadcasted_iota(jnp.int32, sc.shape, 0).astype(F32)
            own = ((i * tq + lax.broadcasted_iota(jnp.int32, sc.shape, 1)) // MOBA_BLOCK).astype(F32)
            sc = jnp.where(n_id < own, sc, NEG)
            keep = n_id == own
            for _ in range(MOBA_TOPK):
                best = jnp.max(sc, axis=0, keepdims=True)
                cand = (sc == best) & (sc > NEG)
                pick = n_id == jnp.min(jnp.where(cand, n_id, float(LANES)), axis=0, keepdims=True)
                keep = keep | pick
                sc = jnp.where(pick, NEG, sc)
            bias_t = jnp.where(keep, 0.0, MASK)
            rows = [jnp.zeros((HEAD_DIM, tq), F32), bias_t]
            if nb_pad < LANES - HEAD_DIM:
                rows.append(jnp.zeros((LANES - HEAD_DIM - nb_pad, tq), F32))
            qt_scr[h] = qt_b + jnp.concatenate(rows, axis=0).astype(BF16)

    def q_t(h):
        return qt_scr[h] if moba else q_ref[0, h, 0]

    m_scr[...] = jnp.full_like(m_scr, MASK)
    acc_scr[...] = jnp.zeros_like(acc_scr)

    def scores(j, s_scr):
        start = pl.multiple_of(j * tq, tq)
        for h in range(N_HEADS):
            s_scr[h] = jnp.dot(k_ref[0, h, pl.ds(start, tq), :], q_t(h),
                               preferred_element_type=F32)

    def consume(j, s_scr, diag):
        for h in range(N_HEADS):
            st = s_scr[h]
            if diag:
                st = jnp.where(causal, st, MASK)
            m_old = m_scr[h]
            m_new = jnp.maximum(m_old, jnp.max(st, axis=0, keepdims=True))
            pt = jnp.exp2(st - m_new)
            acc_scr[h] = acc_scr[h] * jnp.exp2(m_old - m_new) + jnp.dot(
                vt_ref[0, h, j], pt.astype(BF16), preferred_element_type=F32)
            m_scr[h] = m_new

    scores(0, sa_scr)

    def pair(jj, carry):
        j = 2 * jj
        scores(j + 1, sb_scr)
        consume(j, sa_scr, False)
        scores(j + 2, sa_scr)
        consume(j + 1, sb_scr, False)
        return carry

    lax.fori_loop(0, i // 2, pair, 0)

    @pl.when(i % 2 == 0)
    def _():
        consume(i, sa_scr, True)

    @pl.when(i % 2 == 1)
    def _():
        scores(i, sb_scr)
        consume(i - 1, sa_scr, False)
        consume(i, sb_scr, True)

    for h in range(N_HEADS):
        acc = acc_scr[h]
        out = acc[0:HEAD_DIM, :] / acc[HEAD_DIM:HEAD_DIM + 1, :]
        o_ref[0, h * HEAD_DIM:(h + 1) * HEAD_DIM, :] = out.astype(BF16)


def _attention(qt, k, vt, tq, moba):
    b, nh, s, _ = k.shape
    assert qt.shape == vt.shape == (b, nh, s // tq, LANES, tq)
    whole = pl.BlockSpec((1, nh, s, LANES), lambda i, j: (i, 0, 0, 0), pipeline_mode=pl.Buffered(1))
    whole_t = pl.BlockSpec((1,) + vt.shape[1:], lambda i, j: (i, 0, 0, 0, 0), pipeline_mode=pl.Buffered(1))
    scratch = [pltpu.VMEM((nh, LANES, tq), BF16), pltpu.VMEM((nh, 1, tq), F32),
               pltpu.VMEM((nh, LANES, tq), F32),
               pltpu.VMEM((nh, tq, tq), F32), pltpu.VMEM((nh, tq, tq), F32)]
    if moba:
        scratch += [pltpu.VMEM((nh, LANES, LANES), F32), pltpu.VMEM((nh, 3, LANES, LANES), BF16)]
    return pl.pallas_call(
        functools.partial(_attn_kernel, tq=tq, moba=moba), grid=(b, s // tq),
        in_specs=[pl.BlockSpec((1, nh, 1, LANES, tq), lambda i, j: (i, 0, j, 0, 0)), whole, whole_t],
        out_specs=pl.BlockSpec((1, BRANCH, tq), lambda i, j: (i, 0, j)),
        out_shape=jax.ShapeDtypeStruct((b, BRANCH, s), BF16),
        scratch_shapes=scratch,
        compiler_params=_params(2), name="moba_attn" if moba else "fox_attn",
    )(qt, k, vt)


def _merge_kernel(x_ref, ada_ref, u_ref, yp_ref, yc_ref, ym_ref, yf_ref,
                  wg_ref, bg_ref, wbr_ref, wo_ref, lng_ref, lnb_ref, o_ref, *, alpha):
    d = x_ref.shape[-1]
    rows = x_ref.shape[1] // 2
    for r in range(2):
        tok = slice(r * rows, (r + 1) * rows)
        ub = u_ref[0, tok, :]
        merged = None
        for n, y_ref in enumerate((yp_ref, yc_ref, ym_ref, yf_ref)):
            gate = jax.nn.sigmoid(jnp.dot(ub, wg_ref[:, n * d:(n + 1) * d], preferred_element_type=F32)
                                  + bg_ref[:, n * d:(n + 1) * d])
            if n < 2:
                branch = jnp.dot(y_ref[0, tok, :], wbr_ref[n], preferred_element_type=F32)
            else:
                branch = lax.dot_general(y_ref[0, :, tok], wbr_ref[n], _TN, preferred_element_type=F32)
            term = gate * branch
            merged = term if merged is None else merged + term
        mixed = jnp.dot(merged.astype(BF16), wo_ref[...], preferred_element_type=F32)
        res = alpha * x_ref[0, tok, :] + ada_ref[0, 2:3, :] * mixed
        o_ref[0, tok, :] = _layer_norm(res) * lng_ref[...] + lnb_ref[...]


def _merge(x, ada, u, ys, w_gate, b_gate, w_branch, w_o, ln_g, ln_b, ts, alpha, layer):
    b, s, d = x.shape
    tok = lambda width: pl.BlockSpec((1, ts, width), lambda i, j: (i, j, 0))
    return pl.pallas_call(
        functools.partial(_merge_kernel, alpha=alpha), grid=(b, s // ts),
        in_specs=[tok(d), _ada_spec(ada, layer), tok(d), tok(BRANCH), tok(BRANCH)]
                 + [pl.BlockSpec((1, BRANCH, ts), lambda i, j: (i, 0, j))] * 2
                 + [_layer_spec(w, layer) for w in (w_gate, b_gate, w_branch, w_o, ln_g, ln_b)],
        out_specs=tok(d), out_shape=jax.ShapeDtypeStruct((b, s, d), F32),
        compiler_params=_params(2), name="merge_out",
    )(x, ada, u, *ys, w_gate, b_gate, w_branch, w_o, ln_g, ln_b)


def _mlp_kernel(x_ref, ada_ref, wup_ref, bup_ref, wdn_ref, lng_ref, lnb_ref, o_ref, *, alpha, chunk):
    rows = x_ref.shape[1] // 2
    for r in range(2):
        x = x_ref[0, r * rows:(r + 1) * rows, :]
        u2 = (_layer_norm(x) * (1.0 + ada_ref[0, 4:5, :]) + ada_ref[0, 3:4, :]).astype(BF16)
        y = None
        for lo in range(0, wup_ref.shape[1], chunk):
            hid = jnp.dot(u2, wup_ref[:, lo:lo + chunk], preferred_element_type=F32) + bup_ref[:, lo:lo + chunk]
            hid = jnp.square(jnp.maximum(hid, 0.0)).astype(BF16)
            part = jnp.dot(hid, wdn_ref[lo:lo + chunk, :], preferred_element_type=F32)
            y = part if y is None else y + part
        res = alpha * x + ada_ref[0, 5:6, :] * y
        o_ref[0, r * rows:(r + 1) * rows, :] = _layer_norm(res) * lng_ref[...] + lnb_ref[...]


def _mlp(x, ada, w_up, b_up, w_down, ln_g, ln_b, ts, alpha, layer, chunk=1024):
    b, s, d = x.shape
    tok = pl.BlockSpec((1, ts, d), lambda i, j: (i, j, 0))
    return pl.pallas_call(
        functools.partial(_mlp_kernel, alpha=alpha, chunk=chunk), grid=(b, s // ts),
        in_specs=[tok, _ada_spec(ada, layer)]
                 + [_layer_spec(w, layer) for w in (w_up, b_up, w_down, ln_g, ln_b)],
        out_specs=tok, out_shape=jax.ShapeDtypeStruct((b, s, d), F32),
        compiler_params=_params(2), name="mlp",
    )(x, ada, w_up, b_up, w_down, ln_g, ln_b)


def _tile(s):
    for ts in (512, 256):
        if s % ts == 0:
            return ts
    raise ValueError(f"sequence length {s} must be a multiple of {MOBA_BLOCK}")


def kernel(x, c, positions, w_ada, b_ada, w_in, b_in, w_pool, pool_scale, conv_w, w_branch, w_o,
           ln1_g, ln1_b, w_up, b_up, w_down, ln2_g, ln2_b):
    depth = w_ada.shape[0]
    b, s, d = x.shape
    assert d % LANES == 0 and w_in.shape[2] == N_MAIN + N_HEADS + N_HEADS * d
    assert s // MOBA_BLOCK <= LANES - HEAD_DIM
    ts = _tile(s)
    alpha = float((2 * depth) ** 0.25)

    rope = _rope_tables(positions, ts)
    ada_all = _ada_all(c, w_ada, b_ada)
    tri = jnp.tril(jnp.ones((ts, ts), BF16))

    gate0 = N_MAIN + N_HEADS
    pad = LANES - 3 * N_HEADS
    w_f, b_f = w_in[:, :, N_MAIN:gate0], b_in[:, N_MAIN:gate0]
    w_main = w_in[:, :, :N_MAIN].astype(BF16)
    b_main = b_in[:, None, :N_MAIN]
    w_fgt = jnp.concatenate([w_f, w_f, w_f, jnp.zeros((depth, d, pad), F32)], axis=2).astype(BF16)
    b_fgt = jnp.concatenate([b_f, b_f, b_f, jnp.zeros((depth, pad), F32)], axis=1)[:, None, :]
    w_gate = w_in[:, :, gate0:].astype(BF16)
    b_gate = b_in[:, None, gate0:]
    w_poolbd = jnp.stack([jax.scipy.linalg.block_diag(*[w_pool[l, g] for g in range(w_pool.shape[1])])
                          for l in range(depth)]).astype(BF16)
    row3 = lambda a: a[:, None, :]
    w_branch_b, w_o_b, w_up_b, w_down_b = (w.astype(BF16) for w in (w_branch, w_o, w_up, w_down))

    for l in range(depth):
        u, y_pool, y_conv, qm, km, vm, qf, kf, vf = _inproj(
            x, ada_all, w_main, b_main, w_fgt, b_fgt, w_poolbd, row3(pool_scale), conv_w, rope, tri, ts, l)
        y_moba = _attention(qm, km, vm, ts, moba=True)
        y_fox = _attention(qf, kf, vf, ts, moba=False)
        x = _merge(x, ada_all, u, (y_pool, y_conv, y_moba, y_fox), w_gate, b_gate, w_branch_b, w_o_b,
                   row3(ln1_g), row3(ln1_b), ts, alpha, l)
        x = _mlp(x, ada_all, w_up_b, row3(b_up), w_down_b, row3(ln2_g), row3(ln2_b), ts, alpha, l)
    return x
```

```python
import functools

import jax
import jax.numpy as jnp
from jax import lax
from jax.experimental import pallas as pl
from jax.experimental.pallas import tpu as pltpu

F32 = jnp.float32
BF16 = jnp.bfloat16

LANES = 128
HEAD_DIM = 64
N_HEADS = 4
BRANCH = N_HEADS * HEAD_DIM
POOL_WINDOWS = (2, 4, 8, 16)
POOL_HALO = 16
CONV_HALO = 8
MOBA_BLOCK = 256
MOBA_TOPK = 3
ROPE_THETA = 500000.0
ROPE_DIM = HEAD_DIM // 4
LN_EPS = 1e-5
LOG2E = 1.4426950408889634
QSCALE = HEAD_DIM ** -0.5 * LOG2E
MASK = -1e30
NEG = -3e38
N_MAIN = 10 * BRANCH
VMEM_LIMIT = 56 * 1024 * 1024
SUB_TILES = 2

_TN = (((0,), (0,)), ((), ()))


def _params(n_axes):
    return pltpu.CompilerParams(dimension_semantics=("arbitrary",) * n_axes,
                                vmem_limit_bytes=VMEM_LIMIT)


def _const_spec(shape):
    nd = len(shape)
    return pl.BlockSpec(shape, lambda *_: (0,) * nd, pipeline_mode=pl.Buffered(1))


def _layer_spec(stacked, layer):
    nd = stacked.ndim - 1
    return pl.BlockSpec((None,) + stacked.shape[1:], lambda *_: (layer,) + (0,) * nd,
                        pipeline_mode=pl.Buffered(1))


def _ada_spec(ada_all, layer):
    return pl.BlockSpec((None, 1) + ada_all.shape[2:], lambda i, j: (layer, i, 0, 0))


def _layer_norm(x):
    mu = jnp.mean(x, axis=-1, keepdims=True)
    xc = x - mu
    var = jnp.mean(xc * xc, axis=-1, keepdims=True)
    return xc * lax.rsqrt(var + LN_EPS)


def _split3(x):
    hi = x.astype(BF16)
    r1 = x - hi.astype(F32)
    mid = r1.astype(BF16)
    lo = (r1 - mid.astype(F32)).astype(BF16)
    return hi, mid, lo


def _rope_kernel(pos_ref, inv_ref, c_ref, s1_ref, s2_ref):
    ang = pos_ref[0].astype(F32) * inv_ref[...]
    cos, sin = jnp.cos(ang), jnp.sin(ang)
    d = lax.broadcasted_iota(jnp.int32, ang.shape, 1) & (HEAD_DIM - 1)
    half = ROPE_DIM // 2
    c_ref[0] = jnp.where(d < ROPE_DIM, cos, 1.0)
    s1_ref[0] = jnp.where(d < half, -sin, 0.0)
    s2_ref[0] = jnp.where((d >= half) & (d < ROPE_DIM), sin, 0.0)


def _rope_tables(positions, ts):
    b, s = positions.shape
    half = ROPE_DIM // 2
    inv = ROPE_THETA ** (-jnp.arange(0, ROPE_DIM, 2, dtype=F32) / ROPE_DIM)
    inv_row = jnp.tile(inv, LANES // half)[None, :]
    tab = jax.ShapeDtypeStruct((b, s, LANES), F32)
    spec = pl.BlockSpec((1, ts, LANES), lambda i, j: (i, j, 0))
    return pl.pallas_call(
        _rope_kernel, grid=(b, s // ts),
        in_specs=[pl.BlockSpec((1, ts, 1), lambda i, j: (i, j, 0)), _const_spec((1, LANES))],
        out_specs=[spec, spec, spec], out_shape=[tab, tab, tab],
        compiler_params=_params(2), name="rope_tables",
    )(positions[..., None], inv_row)


def _ada_kernel(c_ref, w_ref, b_ref, o_ref):
    c = c_ref[...]
    act = c * jax.nn.sigmoid(c)
    o_ref[...] = jnp.dot(act, w_ref[...], preferred_element_type=F32) + b_ref[...]


def _ada_all(c, w_ada, b_ada, tn=2048):
    depth, d, n = w_ada.shape
    b = c.shape[0]
    rows = -(-b // 8) * 8
    c_pad = jnp.pad(c, ((0, rows - b), (0, 0)))
    out = pl.pallas_call(
        _ada_kernel, grid=(depth, n // tn),
        in_specs=[pl.BlockSpec((rows, d), lambda l, j: (0, 0)),
                  pl.BlockSpec((None, d, tn), lambda l, j: (l, 0, j)),
                  pl.BlockSpec((None, 1, tn), lambda l, j: (l, 0, j))],
        out_specs=pl.BlockSpec((None, rows, tn), lambda l, j: (l, 0, j)),
        out_shape=jax.ShapeDtypeStruct((depth, rows, n), F32),
        compiler_params=_params(2), name="ada_cond",
    )(c_pad, w_ada, b_ada[:, None, :])
    return out[:, :b].reshape(depth, b, 6, d)


def _inproj_kernel(x_ref, ada_ref, w_ref, b_ref, wf_ref, bf_ref, wpool_ref, pscale_ref, convw_ref,
                   rc_ref, rs1_ref, rs2_ref, tri_ref,
                   u_ref, ypool_ref, yconv_ref, qm_ref, km_ref, vm_ref, qf_ref, kf_ref, vf_ref,
                   pbuf, vbuf, ccarry, *, ts):
    t = pl.program_id(1)

    @pl.when(t == 0)
    def _():
        pbuf[0:POOL_HALO, :] = jnp.zeros((POOL_HALO, BRANCH), F32)
        vbuf[0:CONV_HALO, :] = jnp.zeros((CONV_HALO, BRANCH), F32)
        ccarry[...] = jnp.zeros_like(ccarry)

    n = ts // SUB_TILES

    def row_group(r0):
        rows = slice(r0, r0 + n)
        u = _layer_norm(x_ref[0, rows, :]) * (1.0 + ada_ref[0, 1:2, :]) + ada_ref[0, 0:1, :]
        ub = u.astype(BF16)
        u_ref[0, rows, :] = ub

        def proj(lo, hi):
            return jnp.dot(ub, w_ref[:, lo:hi], preferred_element_type=F32) + b_ref[:, lo:hi]

        lane = lax.broadcasted_iota(jnp.int32, (n, LANES), 1)
        row = lax.broadcasted_iota(jnp.int32, (n, 1), 0)
        first = lane < HEAD_DIM

        p_in = proj(0, BRANCH)
        pbuf[POOL_HALO:POOL_HALO + n, :] = p_in
        posp1 = (t * ts + r0 + row + 1).astype(F32)

        def window_sums(lo):
            sums, acc = {}, pbuf[0:POOL_HALO + n, lo:lo + LANES]
            for w in (1, 2, 4, 8):
                acc = acc + pltpu.roll(acc, w, 0)
                sums[2 * w] = acc[POOL_HALO:, :]
            return sums

        pooled = []
        for pair in range(2):
            w_a, w_b = POOL_WINDOWS[2 * pair], POOL_WINDOWS[2 * pair + 1]
            lo = pair * LANES
            sums = window_sums(lo)
            win = jnp.where(first, sums[w_a], sums[w_b])
            cnt = jnp.where(first, jnp.minimum(posp1, float(w_a)), jnp.minimum(posp1, float(w_b)))
            pooled.append(win / cnt - p_in[:, lo:lo + LANES])
        pooled = jnp.concatenate(pooled, axis=-1).astype(BF16)
        y_pool = jnp.dot(pooled, wpool_ref[...], preferred_element_type=F32) * pscale_ref[...]
        ypool_ref[0, rows, :] = y_pool.astype(BF16)
        pbuf[0:POOL_HALO, :] = pbuf[n:n + POOL_HALO, :]

        conv = proj(BRANCH, 4 * BRANCH)
        v = conv[:, BRANCH:2 * BRANCH] * conv[:, 2 * BRANCH:3 * BRANCH]
        vbuf[CONV_HALO:CONV_HALO + n, :] = v
        v_hist = vbuf[0:CONV_HALO + n, :]
        z = (convw_ref[2:3, :] * v
             + convw_ref[1:2, :] * pltpu.roll(v_hist, 1, 0)[CONV_HALO:, :]
             + convw_ref[0:1, :] * pltpu.roll(v_hist, 2, 0)[CONV_HALO:, :])
        yconv_ref[0, rows, :] = (conv[:, 0:BRANCH] * z).astype(BF16)
        vbuf[0:CONV_HALO, :] = vbuf[n:n + CONV_HALO, :]

        rc, rs1, rs2 = rc_ref[0, rows, :], rs1_ref[0, rows, :], rs2_ref[0, rows, :]
        half = ROPE_DIM // 2

        def rope_pairs(a):
            out = []
            for pair in range(2):
                ah = a[:, pair * LANES:(pair + 1) * LANES]
                out.append(ah * rc + pltpu.roll(ah, LANES - half, 1) * rs1 + pltpu.roll(ah, half, 1) * rs2)
            return out

        def plain_pairs(a):
            return [a[:, 0:LANES], a[:, LANES:2 * LANES]]

        def head(pairs, h):
            a = pairs[h // 2]
            return pltpu.roll(a, HEAD_DIM, 1) if h % 2 else a

        ones_col = jnp.where(lane == HEAD_DIM, 1.0, 0.0)

        moba = proj(4 * BRANCH, 7 * BRANCH)
        mq = rope_pairs(moba[:, 0:BRANCH])
        mk = rope_pairs(moba[:, BRANCH:2 * BRANCH])
        mv = plain_pairs(moba[:, 2 * BRANCH:3 * BRANCH])
        blk = (t * ts + r0 + row) // MOBA_BLOCK
        blk_onehot = jnp.where(lane == HEAD_DIM + blk, 1.0, 0.0)
        for h in range(N_HEADS):
            qm_ref[0, h, 0, :, rows] = jnp.where(first, head(mq, h) * QSCALE, 0.0).T.astype(BF16)
            km_ref[0, h, rows, :] = jnp.where(first, head(mk, h), blk_onehot).astype(BF16)
            vm_ref[0, h, 0, :, rows] = jnp.where(first, head(mv, h), ones_col).T.astype(BF16)

        fl = jnp.dot(ub, wf_ref[...], preferred_element_type=F32) + bf_ref[...]
        log_f = jnp.minimum(fl, 0.0) - jnp.log1p(jnp.exp(-jnp.abs(fl)))
        hi, mid, lo = _split3(log_f)
        zero = jnp.zeros_like(hi)
        pieces = jnp.where(lane < N_HEADS, hi, jnp.where(lane < 2 * N_HEADS, mid,
                                                          jnp.where(lane < 3 * N_HEADS, lo, zero)))
        cum = jnp.dot(tri_ref[0:n, 0:n], pieces, preferred_element_type=F32)

        fox = proj(7 * BRANCH, 10 * BRANCH)
        fq = plain_pairs(fox[:, 0:BRANCH])
        fk = plain_pairs(fox[:, BRANCH:2 * BRANCH])
        fv = plain_pairs(fox[:, 2 * BRANCH:3 * BRANCH])
        one_q = jnp.where((lane >= HEAD_DIM + 3) & (lane < HEAD_DIM + 6), 1.0, 0.0)
        one_k = jnp.where((lane >= HEAD_DIM) & (lane < HEAD_DIM + 3), 1.0, 0.0)
        for h in range(N_HEADS):
            mine = (lane < 3 * N_HEADS) & ((lane & (N_HEADS - 1)) == h)
            c_h = jnp.sum(jnp.where(mine, cum, 0.0), axis=-1, keepdims=True) + ccarry[h:h + 1, 0:1]
            ccarry[h:h + 1, :] = jnp.broadcast_to(c_h[n - 1:n, :], (1, LANES))
            c_hi, c_mid, c_lo = (p.astype(F32) for p in _split3(c_h * LOG2E))
            cq = jnp.where(lane == HEAD_DIM, c_hi, jnp.where(lane == HEAD_DIM + 1, c_mid,
                                                              jnp.where(lane == HEAD_DIM + 2, c_lo, one_q)))
            ck = jnp.where(lane == HEAD_DIM + 3, -c_hi, jnp.where(lane == HEAD_DIM + 4, -c_mid,
                                                                   jnp.where(lane == HEAD_DIM + 5, -c_lo, one_k)))
            qf_ref[0, h, 0, :, rows] = jnp.where(first, head(fq, h) * QSCALE, cq).T.astype(BF16)
            kf_ref[0, h, rows, :] = jnp.where(first, head(fk, h), ck).astype(BF16)
            vf_ref[0, h, 0, :, rows] = jnp.where(first, head(fv, h), ones_col).T.astype(BF16)

    for r in range(SUB_TILES):
        row_group(r * n)


def _inproj(x, ada, w_main, b_main, w_fgt, b_fgt, w_poolbd, pool_scale, conv_w, rope, tri, ts, layer):
    b, s, d = x.shape
    tok = lambda width: pl.BlockSpec((1, ts, width), lambda i, j: (i, j, 0))
    heads = pl.BlockSpec((1, N_HEADS, ts, LANES), lambda i, j: (i, 0, j, 0))
    heads_shape = jax.ShapeDtypeStruct((b, N_HEADS, s, LANES), BF16)
    heads_t = pl.BlockSpec((1, N_HEADS, 1, LANES, ts), lambda i, j: (i, 0, j, 0, 0))
    heads_t_shape = jax.ShapeDtypeStruct((b, N_HEADS, s // ts, LANES, ts), BF16)
    branch_shape = jax.ShapeDtypeStruct((b, s, BRANCH), BF16)
    return pl.pallas_call(
        functools.partial(_inproj_kernel, ts=ts), grid=(b, s // ts),
        in_specs=[tok(d), _ada_spec(ada, layer)]
                 + [_layer_spec(w, layer) for w in (w_main, b_main, w_fgt, b_fgt, w_poolbd, pool_scale, conv_w)]
                 + [tok(LANES), tok(LANES), tok(LANES), _const_spec((ts, ts))],
        out_specs=[tok(d), tok(BRANCH), tok(BRANCH)] + [heads_t, heads, heads_t] * 2,
        out_shape=[jax.ShapeDtypeStruct((b, s, d), BF16), branch_shape, branch_shape]
                  + [heads_t_shape, heads_shape, heads_t_shape] * 2,
        scratch_shapes=[pltpu.VMEM((ts + POOL_HALO, BRANCH), F32),
                        pltpu.VMEM((ts + CONV_HALO, BRANCH), F32),
                        pltpu.VMEM((8, LANES), F32)],
        compiler_params=_params(2), name="inproj_mixers",
    )(x, ada, w_main, b_main, w_fgt, b_fgt, w_poolbd, pool_scale, conv_w, *rope, tri)


def _attn_kernel(q_ref, k_ref, vt_ref, o_ref, qt_scr, m_scr, acc_scr, sa_scr, sb_scr, *rest, tq, moba):
    i = pl.program_id(1)
    seq = k_ref.shape[2]
    n_blk = seq // MOBA_BLOCK
    nb_pad = -(-n_blk // 8) * 8

    if moba:
        kbar_scr, kb_scr = rest

        @pl.when(i == 0)
        def _():
            kbar_scr[...] = jnp.zeros_like(kbar_scr)
            lane1 = lax.broadcasted_iota(jnp.int32, (1, LANES), 1)
            for h in range(N_HEADS):
                def block_mean(n, carry):
                    start = pl.multiple_of(n * MOBA_BLOCK, MOBA_BLOCK)
                    kb = k_ref[0, h, pl.ds(start, MOBA_BLOCK), :].astype(F32)
                    mean = jnp.sum(kb, axis=0, keepdims=True) * (1.0 / MOBA_BLOCK)
                    kbar_scr[h, pl.ds(HEAD_DIM + n, 1), :] = jnp.where(lane1 < HEAD_DIM, mean, 0.0)
                    return carry
                lax.fori_loop(0, n_blk, block_mean, 0)
                for piece, val in enumerate(_split3(kbar_scr[h])):
                    kb_scr[h, piece] = val

    key = lax.broadcasted_iota(jnp.int32, (tq, tq), 0)
    qry = lax.broadcasted_iota(jnp.int32, (tq, tq), 1)
    causal = key <= qry

    if moba:
        for h in range(N_HEADS):
            qt_b = q_ref[0, h, 0]
            sc = None
            for piece in range(3):
                part = jnp.dot(kb_scr[h, piece], qt_b, preferred_element_type=F32)
                sc = part if sc is None else sc + part
            sc = sc[HEAD_DIM:HEAD_DIM + nb_pad, :]
            n_id = lax.broadcasted_iota(jnp.int32, sc.shape, 0).astype(F32)
            own = ((i * tq + lax.broadcasted_iota(jnp.int32, sc.shape, 1)) // MOBA_BLOCK).astype(F32)
            sc = jnp.where(n_id < own, sc, NEG)
            keep = n_id == own
            for _ in range(MOBA_TOPK):
                best = jnp.max(sc, axis=0, keepdims=True)
                cand = (sc == best) & (sc > NEG)
                pick = n_id == jnp.min(jnp.where(cand, n_id, float(LANES)), axis=0, keepdims=True)
                keep = keep | pick
                sc = jnp.where(pick, NEG, sc)
            bias_t = jnp.where(keep, 0.0, MASK)
            rows = [jnp.zeros((HEAD_DIM, tq), F32), bias_t]
            if nb_pad < LANES - HEAD_DIM:
                rows.append(jnp.zeros((LANES - HEAD_DIM - nb_pad, tq), F32))
            qt_scr[h] = qt_b + jnp.concatenate(rows, axis=0).astype(BF16)

    def q_t(h):
        return qt_scr[h] if moba else q_ref[0, h, 0]

    m_scr[...] = jnp.full_like(m_scr, MASK)
    acc_scr[...] = jnp.zeros_like(acc_scr)

    def scores(j, s_scr):
        start = pl.multiple_of(j * tq, tq)
        for h in range(N_HEADS):
            s_scr[h] = jnp.dot(k_ref[0, h, pl.ds(start, tq), :], q_t(h),
                               preferred_element_type=F32)

    def consume(j, s_scr, diag):
        for h in range(N_HEADS):
            st = s_scr[h]
            if diag:
                st = jnp.where(causal, st, MASK)
            m_old = m_scr[h]
            m_new = jnp.maximum(m_old, jnp.max(st, axis=0, keepdims=True))
            pt = jnp.exp2(st - m_new)
            acc_scr[h] = acc_scr[h] * jnp.exp2(m_old - m_new) + jnp.dot(
                vt_ref[0, h, j], pt.astype(BF16), preferred_element_type=F32)
            m_scr[h] = m_new

    scores(0, sa_scr)

    def pair(jj, carry):
        j = 2 * jj
        scores(j + 1, sb_scr)
        consume(j, sa_scr, False)
        scores(j + 2, sa_scr)
        consume(j + 1, sb_scr, False)
        return carry

    lax.fori_loop(0, i // 2, pair, 0)

    @pl.when(i % 2 == 0)
    def _():
        consume(i, sa_scr, True)

    @pl.when(i % 2 == 1)
    def _():
        scores(i, sb_scr)
        consume(i - 1, sa_scr, False)
        consume(i, sb_scr, True)

    for h in range(N_HEADS):
        acc = acc_scr[h]
        out = acc[0:HEAD_DIM, :] / acc[HEAD_DIM:HEAD_DIM + 1, :]
        o_ref[0, h * HEAD_DIM:(h + 1) * HEAD_DIM, :] = out.astype(BF16)


def _attention(qt, k, vt, tq, moba):
    b, nh, s, _ = k.shape
    assert qt.shape == vt.shape == (b, nh, s // tq, LANES, tq)
    whole = pl.BlockSpec((1, nh, s, LANES), lambda i, j: (i, 0, 0, 0), pipeline_mode=pl.Buffered(1))
    whole_t = pl.BlockSpec((1,) + vt.shape[1:], lambda i, j: (i, 0, 0, 0, 0), pipeline_mode=pl.Buffered(1))
    scratch = [pltpu.VMEM((nh, LANES, tq), BF16), pltpu.VMEM((nh, 1, tq), F32),
               pltpu.VMEM((nh, LANES, tq), F32),
               pltpu.VMEM((nh, tq, tq), F32), pltpu.VMEM((nh, tq, tq), F32)]
    if moba:
        scratch += [pltpu.VMEM((nh, LANES, LANES), F32), pltpu.VMEM((nh, 3, LANES, LANES), BF16)]
    return pl.pallas_call(
        functools.partial(_attn_kernel, tq=tq, moba=moba), grid=(b, s // tq),
        in_specs=[pl.BlockSpec((1, nh, 1, LANES, tq), lambda i, j: (i, 0, j, 0, 0)), whole, whole_t],
        out_specs=pl.BlockSpec((1, BRANCH, tq), lambda i, j: (i, 0, j)),
        out_shape=jax.ShapeDtypeStruct((b, BRANCH, s), BF16),
        scratch_shapes=scratch,
        compiler_params=_params(2), name="moba_attn" if moba else "fox_attn",
    )(qt, k, vt)


def _merge_kernel(x_ref, ada_ref, u_ref, yp_ref, yc_ref, ym_ref, yf_ref,
                  wg_ref, bg_ref, wbr_ref, wo_ref, lng_ref, lnb_ref, o_ref, *, alpha):
    d = x_ref.shape[-1]
    rows = x_ref.shape[1] // SUB_TILES
    for r in range(SUB_TILES):
        tok = slice(r * rows, (r + 1) * rows)
        ub = u_ref[0, tok, :]
        merged = None
        for n, y_ref in enumerate((yp_ref, yc_ref, ym_ref, yf_ref)):
            gate = jax.nn.sigmoid(jnp.dot(ub, wg_ref[:, n * d:(n + 1) * d], preferred_element_type=F32)
                                  + bg_ref[:, n * d:(n + 1) * d])
            if n < 2:
                branch = jnp.dot(y_ref[0, tok, :], wbr_ref[n], preferred_element_type=F32)
            else:
                branch = lax.dot_general(y_ref[0, :, tok], wbr_ref[n], _TN, preferred_element_type=F32)
            term = gate * branch
            merged = term if merged is None else merged + term
        mixed = jnp.dot(merged.astype(BF16), wo_ref[...], preferred_element_type=F32)
        res = alpha * x_ref[0, tok, :] + ada_ref[0, 2:3, :] * mixed
        o_ref[0, tok, :] = _layer_norm(res) * lng_ref[...] + lnb_ref[...]


def _merge(x, ada, u, ys, w_gate, b_gate, w_branch, w_o, ln_g, ln_b, ts, alpha, layer):
    b, s, d = x.shape
    tok = lambda width: pl.BlockSpec((1, ts, width), lambda i, j: (i, j, 0))
    return pl.pallas_call(
        functools.partial(_merge_kernel, alpha=alpha), grid=(b, s // ts),
        in_specs=[tok(d), _ada_spec(ada, layer), tok(d), tok(BRANCH), tok(BRANCH)]
                 + [pl.BlockSpec((1, BRANCH, ts), lambda i, j: (i, 0, j))] * 2
                 + [_layer_spec(w, layer) for w in (w_gate, b_gate, w_branch, w_o, ln_g, ln_b)],
        out_specs=tok(d), out_shape=jax.ShapeDtypeStruct((b, s, d), F32),
        compiler_params=_params(2), name="merge_out",
    )(x, ada, u, *ys, w_gate, b_gate, w_branch, w_o, ln_g, ln_b)


def _mlp_kernel(x_ref, ada_ref, wup_ref, bup_ref, wdn_ref, lng_ref, lnb_ref, o_ref, *, alpha, chunk):
    rows = x_ref.shape[1] // SUB_TILES
    for r in range(SUB_TILES):
        x = x_ref[0, r * rows:(r + 1) * rows, :]
        u2 = (_layer_norm(x) * (1.0 + ada_ref[0, 4:5, :]) + ada_ref[0, 3:4, :]).astype(BF16)
        y = None
        for lo in range(0, wup_ref.shape[1], chunk):
            hid = jnp.dot(u2, wup_ref[:, lo:lo + chunk], preferred_element_type=F32) + bup_ref[:, lo:lo + chunk]
            hid = jnp.square(jnp.maximum(hid, 0.0)).astype(BF16)
            part = jnp.dot(hid, wdn_ref[lo:lo + chunk, :], preferred_element_type=F32)
            y = part if y is None else y + part
        res = alpha * x + ada_ref[0, 5:6, :] * y
        o_ref[0, r * rows:(r + 1) * rows, :] = _layer_norm(res) * lng_ref[...] + lnb_ref[...]


def _mlp(x, ada, w_up, b_up, w_down, ln_g, ln_b, ts, alpha, layer, chunk=1024):
    b, s, d = x.shape
    tok = pl.BlockSpec((1, ts, d), lambda i, j: (i, j, 0))
    return pl.pallas_call(
        functools.partial(_mlp_kernel, alpha=alpha, chunk=chunk), grid=(b, s // ts),
        in_specs=[tok, _ada_spec(ada, layer)]
                 + [_layer_spec(w, layer) for w in (w_up, b_up, w_down, ln_g, ln_b)],
        out_specs=tok, out_shape=jax.ShapeDtypeStruct((b, s, d), F32),
        compiler_params=_params(2), name="mlp",
    )(x, ada, w_up, b_up, w_down, ln_g, ln_b)


def _tile(s):
    for ts in (512, 256):
        if s % ts == 0:
            return ts
    raise ValueError(f"sequence length {s} must be a multiple of {MOBA_BLOCK}")


def kernel(x, c, positions, w_ada, b_ada, w_in, b_in, w_pool, pool_scale, conv_w, w_branch, w_o,
           ln1_g, ln1_b, w_up, b_up, w_down, ln2_g, ln2_b):
    depth = w_ada.shape[0]
    b, s, d = x.shape
    assert d % LANES == 0 and w_in.shape[2] == N_MAIN + N_HEADS + N_HEADS * d
    assert s // MOBA_BLOCK <= LANES - HEAD_DIM
    ts = _tile(s)
    alpha = float((2 * depth) ** 0.25)

    rope = _rope_tables(positions, ts)
    ada_all = _ada_all(c, w_ada, b_ada)
    tri = jnp.tril(jnp.ones((ts, ts), BF16))

    gate0 = N_MAIN + N_HEADS
    pad = LANES - 3 * N_HEADS
    w_f, b_f = w_in[:, :, N_MAIN:gate0], b_in[:, N_MAIN:gate0]
    w_main = w_in[:, :, :N_MAIN].astype(BF16)
    b_main = b_in[:, None, :N_MAIN]
    w_fgt = jnp.concatenate([w_f, w_f, w_f, jnp.zeros((depth, d, pad), F32)], axis=2).astype(BF16)
    b_fgt = jnp.concatenate([b_f, b_f, b_f, jnp.zeros((depth, pad), F32)], axis=1)[:, None, :]
    w_gate = w_in[:, :, gate0:].astype(BF16)
    b_gate = b_in[:, None, gate0:]
    w_poolbd = jnp.stack([jax.scipy.linalg.block_diag(*[w_pool[l, g] for g in range(w_pool.shape[1])])
                          for l in range(depth)]).astype(BF16)
    row3 = lambda a: a[:, None, :]
    w_branch_b, w_o_b, w_up_b, w_down_b = (w.astype(BF16) for w in (w_branch, w_o, w_up, w_down))

    for l in range(depth):
        u, y_pool, y_conv, qm, km, vm, qf, kf, vf = _inproj(
            x, ada_all, w_main, b_main, w_fgt, b_fgt, w_poolbd, row3(pool_scale), conv_w, rope, tri, ts, l)
        y_moba = _attention(qm, km, vm, ts, moba=True)
        y_fox = _attention(qf, kf, vf, ts, moba=False)
        x = _merge(x, ada_all, u, (y_pool, y_conv, y_moba, y_fox), w_gate, b_gate, w_branch_b, w_o_b,
                   row3(ln1_g), row3(ln1_b), ts, alpha, l)
        x = _mlp(x, ada_all, w_up_b, row3(b_up), w_down_b, row3(ln2_g), row3(ln2_b), ts, alpha, l)
    return x
```

```python
import functools

import jax
import jax.numpy as jnp
from jax import lax
from jax.experimental import pallas as pl
from jax.experimental.pallas import tpu as pltpu

F32 = jnp.float32
BF16 = jnp.bfloat16

LANES = 128
HEAD_DIM = 64
N_HEADS = 4
BRANCH = N_HEADS * HEAD_DIM
POOL_WINDOWS = (2, 4, 8, 16)
POOL_HALO = 16
CONV_HALO = 8
MOBA_BLOCK = 256
MOBA_TOPK = 3
ROPE_THETA = 500000.0
ROPE_DIM = HEAD_DIM // 4
LN_EPS = 1e-5
LOG2E = 1.4426950408889634
QSCALE = HEAD_DIM ** -0.5 * LOG2E
MASK = -1e30
NEG = -3e38
N_MAIN = 10 * BRANCH
VMEM_LIMIT = 56 * 1024 * 1024
GROUP_ROWS = 256

_TN = (((0,), (0,)), ((), ()))


def _params(n_axes):
    return pltpu.CompilerParams(dimension_semantics=("arbitrary",) * n_axes,
                                vmem_limit_bytes=VMEM_LIMIT)


def _const_spec(shape):
    nd = len(shape)
    return pl.BlockSpec(shape, lambda *_: (0,) * nd, pipeline_mode=pl.Buffered(1))


def _layer_spec(stacked, layer):
    nd = stacked.ndim - 1
    return pl.BlockSpec((None,) + stacked.shape[1:], lambda *_: (layer,) + (0,) * nd,
                        pipeline_mode=pl.Buffered(1))


def _ada_spec(ada_all, layer):
    return pl.BlockSpec((None, 1) + ada_all.shape[2:], lambda i, j: (layer, i, 0, 0))


def _layer_norm(x):
    mu = jnp.mean(x, axis=-1, keepdims=True)
    xc = x - mu
    var = jnp.mean(xc * xc, axis=-1, keepdims=True)
    return xc * lax.rsqrt(var + LN_EPS)


def _split3(x):
    hi = x.astype(BF16)
    r1 = x - hi.astype(F32)
    mid = r1.astype(BF16)
    lo = (r1 - mid.astype(F32)).astype(BF16)
    return hi, mid, lo


def _rope_kernel(pos_ref, inv_ref, c_ref, s1_ref, s2_ref):
    ang = pos_ref[0].astype(F32) * inv_ref[...]
    cos, sin = jnp.cos(ang), jnp.sin(ang)
    d = lax.broadcasted_iota(jnp.int32, ang.shape, 1) & (HEAD_DIM - 1)
    half = ROPE_DIM // 2
    c_ref[0] = jnp.where(d < ROPE_DIM, cos, 1.0)
    s1_ref[0] = jnp.where(d < half, -sin, 0.0)
    s2_ref[0] = jnp.where((d >= half) & (d < ROPE_DIM), sin, 0.0)


def _rope_tables(positions, ts):
    b, s = positions.shape
    half = ROPE_DIM // 2
    inv = ROPE_THETA ** (-jnp.arange(0, ROPE_DIM, 2, dtype=F32) / ROPE_DIM)
    inv_row = jnp.tile(inv, LANES // half)[None, :]
    tab = jax.ShapeDtypeStruct((b, s, LANES), F32)
    spec = pl.BlockSpec((1, ts, LANES), lambda i, j: (i, j, 0))
    return pl.pallas_call(
        _rope_kernel, grid=(b, s // ts),
        in_specs=[pl.BlockSpec((1, ts, 1), lambda i, j: (i, j, 0)), _const_spec((1, LANES))],
        out_specs=[spec, spec, spec], out_shape=[tab, tab, tab],
        compiler_params=_params(2), name="rope_tables",
    )(positions[..., None], inv_row)


def _ada_kernel(c_ref, w_ref, b_ref, o_ref):
    c = c_ref[...]
    act = c * jax.nn.sigmoid(c)
    o_ref[...] = jnp.dot(act, w_ref[...], preferred_element_type=F32) + b_ref[...]


def _ada_all(c, w_ada, b_ada, tn=2048):
    depth, d, n = w_ada.shape
    b = c.shape[0]
    rows = -(-b // 8) * 8
    c_pad = jnp.pad(c, ((0, rows - b), (0, 0)))
    out = pl.pallas_call(
        _ada_kernel, grid=(depth, n // tn),
        in_specs=[pl.BlockSpec((rows, d), lambda l, j: (0, 0)),
                  pl.BlockSpec((None, d, tn), lambda l, j: (l, 0, j)),
                  pl.BlockSpec((None, 1, tn), lambda l, j: (l, 0, j))],
        out_specs=pl.BlockSpec((None, rows, tn), lambda l, j: (l, 0, j)),
        out_shape=jax.ShapeDtypeStruct((depth, rows, n), F32),
        compiler_params=_params(2), name="ada_cond",
    )(c_pad, w_ada, b_ada[:, None, :])
    return out[:, :b].reshape(depth, b, 6, d)


def _inproj_kernel(x_ref, ada_ref, w_ref, b_ref, wf_ref, bf_ref, wpool_ref, pscale_ref, convw_ref,
                   rc_ref, rs1_ref, rs2_ref, tri_ref,
                   u_ref, ypool_ref, yconv_ref, qm_ref, km_ref, vm_ref, qf_ref, kf_ref, vf_ref,
                   pbuf, vbuf, ccarry, *, ts):
    t = pl.program_id(1)

    @pl.when(t == 0)
    def _():
        pbuf[0:POOL_HALO, :] = jnp.zeros((POOL_HALO, BRANCH), F32)
        vbuf[0:CONV_HALO, :] = jnp.zeros((CONV_HALO, BRANCH), F32)
        ccarry[...] = jnp.zeros_like(ccarry)

    n = min(GROUP_ROWS, ts)

    def row_group(r0):
        rows = slice(r0, r0 + n)
        u = _layer_norm(x_ref[0, rows, :]) * (1.0 + ada_ref[0, 1:2, :]) + ada_ref[0, 0:1, :]
        ub = u.astype(BF16)
        u_ref[0, rows, :] = ub

        def proj(lo, hi):
            return jnp.dot(ub, w_ref[:, lo:hi], preferred_element_type=F32) + b_ref[:, lo:hi]

        lane = lax.broadcasted_iota(jnp.int32, (n, LANES), 1)
        row = lax.broadcasted_iota(jnp.int32, (n, 1), 0)
        first = lane < HEAD_DIM

        p_in = proj(0, BRANCH)
        pbuf[POOL_HALO:POOL_HALO + n, :] = p_in
        posp1 = (t * ts + r0 + row + 1).astype(F32)

        def window_sums(lo):
            sums, acc = {}, pbuf[0:POOL_HALO + n, lo:lo + LANES]
            for w in (1, 2, 4, 8):
                acc = acc + pltpu.roll(acc, w, 0)
                sums[2 * w] = acc[POOL_HALO:, :]
            return sums

        pooled = []
        for pair in range(2):
            w_a, w_b = POOL_WINDOWS[2 * pair], POOL_WINDOWS[2 * pair + 1]
            lo = pair * LANES
            sums = window_sums(lo)
            win = jnp.where(first, sums[w_a], sums[w_b])
            cnt = jnp.where(first, jnp.minimum(posp1, float(w_a)), jnp.minimum(posp1, float(w_b)))
            pooled.append(win / cnt - p_in[:, lo:lo + LANES])
        pooled = jnp.concatenate(pooled, axis=-1).astype(BF16)
        y_pool = jnp.dot(pooled, wpool_ref[...], preferred_element_type=F32) * pscale_ref[...]
        ypool_ref[0, rows, :] = y_pool.astype(BF16)
        pbuf[0:POOL_HALO, :] = pbuf[n:n + POOL_HALO, :]

        conv = proj(BRANCH, 4 * BRANCH)
        v = conv[:, BRANCH:2 * BRANCH] * conv[:, 2 * BRANCH:3 * BRANCH]
        vbuf[CONV_HALO:CONV_HALO + n, :] = v
        v_hist = vbuf[0:CONV_HALO + n, :]
        z = (convw_ref[2:3, :] * v
             + convw_ref[1:2, :] * pltpu.roll(v_hist, 1, 0)[CONV_HALO:, :]
             + convw_ref[0:1, :] * pltpu.roll(v_hist, 2, 0)[CONV_HALO:, :])
        yconv_ref[0, rows, :] = (conv[:, 0:BRANCH] * z).astype(BF16)
        vbuf[0:CONV_HALO, :] = vbuf[n:n + CONV_HALO, :]

        rc, rs1, rs2 = rc_ref[0, rows, :], rs1_ref[0, rows, :], rs2_ref[0, rows, :]
        half = ROPE_DIM // 2

        def rope_pairs(a):
            out = []
            for pair in range(2):
                ah = a[:, pair * LANES:(pair + 1) * LANES]
                out.append(ah * rc + pltpu.roll(ah, LANES - half, 1) * rs1 + pltpu.roll(ah, half, 1) * rs2)
            return out

        def plain_pairs(a):
            return [a[:, 0:LANES], a[:, LANES:2 * LANES]]

        def head(pairs, h):
            a = pairs[h // 2]
            return pltpu.roll(a, HEAD_DIM, 1) if h % 2 else a

        ones_col = jnp.where(lane == HEAD_DIM, 1.0, 0.0)

        moba = proj(4 * BRANCH, 7 * BRANCH)
        mq = rope_pairs(moba[:, 0:BRANCH])
        mk = rope_pairs(moba[:, BRANCH:2 * BRANCH])
        mv = plain_pairs(moba[:, 2 * BRANCH:3 * BRANCH])
        blk = (t * ts + r0 + row) // MOBA_BLOCK
        blk_onehot = jnp.where(lane == HEAD_DIM + blk, 1.0, 0.0)
        for h in range(N_HEADS):
            qm_ref[0, h, 0, :, rows] = jnp.where(first, head(mq, h) * QSCALE, 0.0).T.astype(BF16)
            km_ref[0, h, rows, :] = jnp.where(first, head(mk, h), blk_onehot).astype(BF16)
            vm_ref[0, h, 0, :, rows] = jnp.where(first, head(mv, h), ones_col).T.astype(BF16)

        fl = jnp.dot(ub, wf_ref[...], preferred_element_type=F32) + bf_ref[...]
        log_f = jnp.minimum(fl, 0.0) - jnp.log1p(jnp.exp(-jnp.abs(fl)))
        hi, mid, lo = _split3(log_f)
        zero = jnp.zeros_like(hi)
        pieces = jnp.where(lane < N_HEADS, hi, jnp.where(lane < 2 * N_HEADS, mid,
                                                          jnp.where(lane < 3 * N_HEADS, lo, zero)))
        cum = jnp.dot(tri_ref[0:n, 0:n], pieces, preferred_element_type=F32)

        fox = proj(7 * BRANCH, 10 * BRANCH)
        fq = plain_pairs(fox[:, 0:BRANCH])
        fk = plain_pairs(fox[:, BRANCH:2 * BRANCH])
        fv = plain_pairs(fox[:, 2 * BRANCH:3 * BRANCH])
        one_q = jnp.where((lane >= HEAD_DIM + 3) & (lane < HEAD_DIM + 6), 1.0, 0.0)
        one_k = jnp.where((lane >= HEAD_DIM) & (lane < HEAD_DIM + 3), 1.0, 0.0)
        for h in range(N_HEADS):
            mine = (lane < 3 * N_HEADS) & ((lane & (N_HEADS - 1)) == h)
            c_h = jnp.sum(jnp.where(mine, cum, 0.0), axis=-1, keepdims=True) + ccarry[h:h + 1, 0:1]
            ccarry[h:h + 1, :] = jnp.broadcast_to(c_h[n - 1:n, :], (1, LANES))
            c_hi, c_mid, c_lo = (p.astype(F32) for p in _split3(c_h * LOG2E))
            cq = jnp.where(lane == HEAD_DIM, c_hi, jnp.where(lane == HEAD_DIM + 1, c_mid,
                                                              jnp.where(lane == HEAD_DIM + 2, c_lo, one_q)))
            ck = jnp.where(lane == HEAD_DIM + 3, -c_hi, jnp.where(lane == HEAD_DIM + 4, -c_mid,
                                                                   jnp.where(lane == HEAD_DIM + 5, -c_lo, one_k)))
            qf_ref[0, h, 0, :, rows] = jnp.where(first, head(fq, h) * QSCALE, cq).T.astype(BF16)
            kf_ref[0, h, rows, :] = jnp.where(first, head(fk, h), ck).astype(BF16)
            vf_ref[0, h, 0, :, rows] = jnp.where(first, head(fv, h), ones_col).T.astype(BF16)

    for r in range(ts // n):
        row_group(r * n)


def _inproj(x, ada, w_main, b_main, w_fgt, b_fgt, w_poolbd, pool_scale, conv_w, rope, tri, ts, layer):
    b, s, d = x.shape
    tok = lambda width: pl.BlockSpec((1, ts, width), lambda i, j: (i, j, 0))
    heads = pl.BlockSpec((1, N_HEADS, ts, LANES), lambda i, j: (i, 0, j, 0))
    heads_shape = jax.ShapeDtypeStruct((b, N_HEADS, s, LANES), BF16)
    heads_t = pl.BlockSpec((1, N_HEADS, 1, LANES, ts), lambda i, j: (i, 0, j, 0, 0))
    heads_t_shape = jax.ShapeDtypeStruct((b, N_HEADS, s // ts, LANES, ts), BF16)
    branch_shape = jax.ShapeDtypeStruct((b, s, BRANCH), BF16)
    return pl.pallas_call(
        functools.partial(_inproj_kernel, ts=ts), grid=(b, s // ts),
        in_specs=[tok(d), _ada_spec(ada, layer)]
                 + [_layer_spec(w, layer) for w in (w_main, b_main, w_fgt, b_fgt, w_poolbd, pool_scale, conv_w)]
                 + [tok(LANES), tok(LANES), tok(LANES), _const_spec((ts, ts))],
        out_specs=[tok(d), tok(BRANCH), tok(BRANCH)] + [heads_t, heads, heads_t] * 2,
        out_shape=[jax.ShapeDtypeStruct((b, s, d), BF16), branch_shape, branch_shape]
                  + [heads_t_shape, heads_shape, heads_t_shape] * 2,
        scratch_shapes=[pltpu.VMEM((ts + POOL_HALO, BRANCH), F32),
                        pltpu.VMEM((ts + CONV_HALO, BRANCH), F32),
                        pltpu.VMEM((8, LANES), F32)],
        compiler_params=_params(2), name="inproj_mixers",
    )(x, ada, w_main, b_main, w_fgt, b_fgt, w_poolbd, pool_scale, conv_w, *rope, tri)


def _attn_kernel(q_ref, k_ref, vt_ref, o_ref, qt_scr, m_scr, acc_scr, sa_scr, sb_scr, *rest, tq, moba):
    i = pl.program_id(1)
    seq = k_ref.shape[2]
    n_blk = seq // MOBA_BLOCK
    nb_pad = -(-n_blk // 8) * 8

    if moba:
        kbar_scr, kb_scr = rest

        @pl.when(i == 0)
        def _():
            kbar_scr[...] = jnp.zeros_like(kbar_scr)
            lane1 = lax.broadcasted_iota(jnp.int32, (1, LANES), 1)
            for h in range(N_HEADS):
                def block_mean(n, carry):
                    start = pl.multiple_of(n * MOBA_BLOCK, MOBA_BLOCK)
                    kb = k_ref[0, h, pl.ds(start, MOBA_BLOCK), :].astype(F32)
                    mean = jnp.sum(kb, axis=0, keepdims=True) * (1.0 / MOBA_BLOCK)
                    kbar_scr[h, pl.ds(HEAD_DIM + n, 1), :] = jnp.where(lane1 < HEAD_DIM, mean, 0.0)
                    return carry
                lax.fori_loop(0, n_blk, block_mean, 0)
                for piece, val in enumerate(_split3(kbar_scr[h])):
                    kb_scr[h, piece] = val

    key = lax.broadcasted_iota(jnp.int32, (tq, tq), 0)
    qry = lax.broadcasted_iota(jnp.int32, (tq, tq), 1)
    causal = key <= qry

    if moba:
        for h in range(N_HEADS):
            qt_b = q_ref[0, h, 0]
            sc = None
            for piece in range(3):
                part = jnp.dot(kb_scr[h, piece], qt_b, preferred_element_type=F32)
                sc = part if sc is None else sc + part
            sc = sc[HEAD_DIM:HEAD_DIM + nb_pad, :]
            n_id = lax.broadcasted_iota(jnp.int32, sc.shape, 0).astype(F32)
            own = ((i * tq + lax.broadcasted_iota(jnp.int32, sc.shape, 1)) // MOBA_BLOCK).astype(F32)
            sc = jnp.where(n_id < own, sc, NEG)
            keep = n_id == own
            for _ in range(MOBA_TOPK):
                best = jnp.max(sc, axis=0, keepdims=True)
                cand = (sc == best) & (sc > NEG)
                pick = n_id == jnp.min(jnp.where(cand, n_id, float(LANES)), axis=0, keepdims=True)
                keep = keep | pick
                sc = jnp.where(pick, NEG, sc)
            bias_t = jnp.where(keep, 0.0, MASK)
            rows = [jnp.zeros((HEAD_DIM, tq), F32), bias_t]
            if nb_pad < LANES - HEAD_DIM:
                rows.append(jnp.zeros((LANES - HEAD_DIM - nb_pad, tq), F32))
            qt_scr[h] = qt_b + jnp.concatenate(rows, axis=0).astype(BF16)

    def q_t(h):
        return qt_scr[h] if moba else q_ref[0, h, 0]

    m_scr[...] = jnp.full_like(m_scr, MASK)
    acc_scr[...] = jnp.zeros_like(acc_scr)

    def scores(j, s_scr):
        start = pl.multiple_of(j * tq, tq)
        for h in range(N_HEADS):
            s_scr[h] = jnp.dot(k_ref[0, h, pl.ds(start, tq), :], q_t(h),
                               preferred_element_type=F32)

    def consume(j, s_scr, diag):
        for h in range(N_HEADS):
            st = s_scr[h]
            if diag:
                st = jnp.where(causal, st, MASK)
            m_old = m_scr[h]
            m_new = jnp.maximum(m_old, jnp.max(st, axis=0, keepdims=True))
            pt = jnp.exp2(st - m_new)
            acc_scr[h] = acc_scr[h] * jnp.exp2(m_old - m_new) + jnp.dot(
                vt_ref[0, h, j], pt.astype(BF16), preferred_element_type=F32)
            m_scr[h] = m_new

    scores(0, sa_scr)

    def pair(jj, carry):
        j = 2 * jj
        scores(j + 1, sb_scr)
        consume(j, sa_scr, False)
        scores(j + 2, sa_scr)
        consume(j + 1, sb_scr, False)
        return carry

    lax.fori_loop(0, i // 2, pair, 0)

    @pl.when(i % 2 == 0)
    def _():
        consume(i, sa_scr, True)

    @pl.when(i % 2 == 1)
    def _():
        scores(i, sb_scr)
        consume(i - 1, sa_scr, False)
        consume(i, sb_scr, True)

    for h in range(N_HEADS):
        acc = acc_scr[h]
        out = acc[0:HEAD_DIM, :] / acc[HEAD_DIM:HEAD_DIM + 1, :]
        o_ref[0, h * HEAD_DIM:(h + 1) * HEAD_DIM, :] = out.astype(BF16)


def _attention(qt, k, vt, tq, moba):
    b, nh, s, _ = k.shape
    assert qt.shape == vt.shape == (b, nh, s // tq, LANES, tq)
    whole = pl.BlockSpec((1, nh, s, LANES), lambda i, j: (i, 0, 0, 0), pipeline_mode=pl.Buffered(1))
    whole_t = pl.BlockSpec((1,) + vt.shape[1:], lambda i, j: (i, 0, 0, 0, 0), pipeline_mode=pl.Buffered(1))
    scratch = [pltpu.VMEM((nh, LANES, tq), BF16), pltpu.VMEM((nh, 1, tq), F32),
               pltpu.VMEM((nh, LANES, tq), F32),
               pltpu.VMEM((nh, tq, tq), F32), pltpu.VMEM((nh, tq, tq), F32)]
    if moba:
        scratch += [pltpu.VMEM((nh, LANES, LANES), F32), pltpu.VMEM((nh, 3, LANES, LANES), BF16)]
    return pl.pallas_call(
        functools.partial(_attn_kernel, tq=tq, moba=moba), grid=(b, s // tq),
        in_specs=[pl.BlockSpec((1, nh, 1, LANES, tq), lambda i, j: (i, 0, j, 0, 0)), whole, whole_t],
        out_specs=pl.BlockSpec((1, BRANCH, tq), lambda i, j: (i, 0, j)),
        out_shape=jax.ShapeDtypeStruct((b, BRANCH, s), BF16),
        scratch_shapes=scratch,
        compiler_params=_params(2), name="moba_attn" if moba else "fox_attn",
    )(qt, k, vt)


def _merge_kernel(x_ref, ada_ref, u_ref, yp_ref, yc_ref, ym_ref, yf_ref,
                  wg_ref, bg_ref, wbr_ref, wo_ref, lng_ref, lnb_ref, o_ref, *, alpha):
    d = x_ref.shape[-1]
    rows = min(GROUP_ROWS, x_ref.shape[1])
    for r in range(x_ref.shape[1] // rows):
        tok = slice(r * rows, (r + 1) * rows)
        ub = u_ref[0, tok, :]
        merged = None
        for n, y_ref in enumerate((yp_ref, yc_ref, ym_ref, yf_ref)):
            gate = jax.nn.sigmoid(jnp.dot(ub, wg_ref[:, n * d:(n + 1) * d], preferred_element_type=F32)
                                  + bg_ref[:, n * d:(n + 1) * d])
            if n < 2:
                branch = jnp.dot(y_ref[0, tok, :], wbr_ref[n], preferred_element_type=F32)
            else:
                branch = lax.dot_general(y_ref[0, :, tok], wbr_ref[n], _TN, preferred_element_type=F32)
            term = gate * branch
            merged = term if merged is None else merged + term
        mixed = jnp.dot(merged.astype(BF16), wo_ref[...], preferred_element_type=F32)
        res = alpha * x_ref[0, tok, :] + ada_ref[0, 2:3, :] * mixed
        o_ref[0, tok, :] = _layer_norm(res) * lng_ref[...] + lnb_ref[...]


def _merge(x, ada, u, ys, w_gate, b_gate, w_branch, w_o, ln_g, ln_b, ts, alpha, layer):
    b, s, d = x.shape
    tok = lambda width: pl.BlockSpec((1, ts, width), lambda i, j: (i, j, 0))
    return pl.pallas_call(
        functools.partial(_merge_kernel, alpha=alpha), grid=(b, s // ts),
        in_specs=[tok(d), _ada_spec(ada, layer), tok(d), tok(BRANCH), tok(BRANCH)]
                 + [pl.BlockSpec((1, BRANCH, ts), lambda i, j: (i, 0, j))] * 2
                 + [_layer_spec(w, layer) for w in (w_gate, b_gate, w_branch, w_o, ln_g, ln_b)],
        out_specs=tok(d), out_shape=jax.ShapeDtypeStruct((b, s, d), F32),
        compiler_params=_params(2), name="merge_out",
    )(x, ada, u, *ys, w_gate, b_gate, w_branch, w_o, ln_g, ln_b)


def _mlp_kernel(x_ref, ada_ref, wup_ref, bup_ref, wdn_ref, lng_ref, lnb_ref, o_ref, *, alpha, chunk):
    rows = min(GROUP_ROWS, x_ref.shape[1])
    for r in range(x_ref.shape[1] // rows):
        x = x_ref[0, r * rows:(r + 1) * rows, :]
        u2 = (_layer_norm(x) * (1.0 + ada_ref[0, 4:5, :]) + ada_ref[0, 3:4, :]).astype(BF16)
        y = None
        for lo in range(0, wup_ref.shape[1], chunk):
            hid = jnp.dot(u2, wup_ref[:, lo:lo + chunk], preferred_element_type=F32) + bup_ref[:, lo:lo + chunk]
            hid = jnp.square(jnp.maximum(hid, 0.0)).astype(BF16)
            part = jnp.dot(hid, wdn_ref[lo:lo + chunk, :], preferred_element_type=F32)
            y = part if y is None else y + part
        res = alpha * x + ada_ref[0, 5:6, :] * y
        o_ref[0, r * rows:(r + 1) * rows, :] = _layer_norm(res) * lng_ref[...] + lnb_ref[...]


def _mlp(x, ada, w_up, b_up, w_down, ln_g, ln_b, ts, alpha, layer, chunk=1024):
    b, s, d = x.shape
    tok = pl.BlockSpec((1, ts, d), lambda i, j: (i, j, 0))
    return pl.pallas_call(
        functools.partial(_mlp_kernel, alpha=alpha, chunk=chunk), grid=(b, s // ts),
        in_specs=[tok, _ada_spec(ada, layer)]
                 + [_layer_spec(w, layer) for w in (w_up, b_up, w_down, ln_g, ln_b)],
        out_specs=tok, out_shape=jax.ShapeDtypeStruct((b, s, d), F32),
        compiler_params=_params(2), name="mlp",
    )(x, ada, w_up, b_up, w_down, ln_g, ln_b)


def _tile(s):
    for ts in (512, 256):
        if s % ts == 0:
            return ts
    raise ValueError(f"sequence length {s} must be a multiple of {MOBA_BLOCK}")


def kernel(x, c, positions, w_ada, b_ada, w_in, b_in, w_pool, pool_scale, conv_w, w_branch, w_o,
           ln1_g, ln1_b, w_up, b_up, w_down, ln2_g, ln2_b):
    depth = w_ada.shape[0]
    b, s, d = x.shape
    assert d % LANES == 0 and w_in.shape[2] == N_MAIN + N_HEADS + N_HEADS * d
    assert s // MOBA_BLOCK <= LANES - HEAD_DIM
    ts = _tile(s)
    ts_wide = 2 * ts if s % (2 * ts) == 0 else ts
    alpha = float((2 * depth) ** 0.25)

    rope = _rope_tables(positions, ts)
    ada_all = _ada_all(c, w_ada, b_ada)
    tri = jnp.tril(jnp.ones((ts, ts), BF16))

    gate0 = N_MAIN + N_HEADS
    pad = LANES - 3 * N_HEADS
    w_f, b_f = w_in[:, :, N_MAIN:gate0], b_in[:, N_MAIN:gate0]
    w_main = w_in[:, :, :N_MAIN].astype(BF16)
    b_main = b_in[:, None, :N_MAIN]
    w_fgt = jnp.concatenate([w_f, w_f, w_f, jnp.zeros((depth, d, pad), F32)], axis=2).astype(BF16)
    b_fgt = jnp.concatenate([b_f, b_f, b_f, jnp.zeros((depth, pad), F32)], axis=1)[:, None, :]
    w_gate = w_in[:, :, gate0:].astype(BF16)
    b_gate = b_in[:, None, gate0:]
    w_poolbd = jnp.stack([jax.scipy.linalg.block_diag(*[w_pool[l, g] for g in range(w_pool.shape[1])])
                          for l in range(depth)]).astype(BF16)
    row3 = lambda a: a[:, None, :]
    w_branch_b, w_o_b, w_up_b, w_down_b = (w.astype(BF16) for w in (w_branch, w_o, w_up, w_down))

    for l in range(depth):
        u, y_pool, y_conv, qm, km, vm, qf, kf, vf = _inproj(
            x, ada_all, w_main, b_main, w_fgt, b_fgt, w_poolbd, row3(pool_scale), conv_w, rope, tri, ts, l)
        y_moba = _attention(qm, km, vm, ts, moba=True)
        y_fox = _attention(qf, kf, vf, ts, moba=False)
        x = _merge(x, ada_all, u, (y_pool, y_conv, y_moba, y_fox), w_gate, b_gate, w_branch_b, w_o_b,
                   row3(ln1_g), row3(ln1_b), ts_wide, alpha, l)
        x = _mlp(x, ada_all, w_up_b, row3(b_up), w_down_b, row3(ln2_g), row3(ln2_b), ts_wide, alpha, l)
    return x
```

```python
import functools

import jax
import jax.numpy as jnp
from jax import lax
from jax.experimental import pallas as pl
from jax.experimental.pallas import tpu as pltpu

F32 = jnp.float32
BF16 = jnp.bfloat16

LANES = 128
HEAD_DIM = 64
N_HEADS = 4
BRANCH = N_HEADS * HEAD_DIM
POOL_WINDOWS = (2, 4, 8, 16)
POOL_HALO = 16
CONV_HALO = 8
MOBA_BLOCK = 256
MOBA_TOPK = 3
ROPE_THETA = 500000.0
ROPE_DIM = HEAD_DIM // 4
LN_EPS = 1e-5
LOG2E = 1.4426950408889634
QSCALE = HEAD_DIM ** -0.5 * LOG2E
MASK = -1e30
NEG = -3e38
N_MAIN = 10 * BRANCH
VMEM_LIMIT = 56 * 1024 * 1024
GROUP_ROWS = 256

_TN = (((0,), (0,)), ((), ()))


def _params(n_axes):
    return pltpu.CompilerParams(dimension_semantics=("arbitrary",) * n_axes,
                                vmem_limit_bytes=VMEM_LIMIT)


def _const_spec(shape):
    nd = len(shape)
    return pl.BlockSpec(shape, lambda *_: (0,) * nd, pipeline_mode=pl.Buffered(1))


def _layer_spec(stacked, layer):
    nd = stacked.ndim - 1
    return pl.BlockSpec((None,) + stacked.shape[1:], lambda *_: (layer,) + (0,) * nd,
                        pipeline_mode=pl.Buffered(1))


def _ada_spec(ada_all, layer):
    return pl.BlockSpec((None, 1) + ada_all.shape[2:], lambda i, j: (layer, i, 0, 0))


def _layer_norm(x):
    mu = jnp.mean(x, axis=-1, keepdims=True)
    xc = x - mu
    var = jnp.mean(xc * xc, axis=-1, keepdims=True)
    return xc * lax.rsqrt(var + LN_EPS)


def _split3(x):
    hi = x.astype(BF16)
    r1 = x - hi.astype(F32)
    mid = r1.astype(BF16)
    lo = (r1 - mid.astype(F32)).astype(BF16)
    return hi, mid, lo


def _rope_kernel(pos_ref, inv_ref, c_ref, s1_ref, s2_ref):
    ang = pos_ref[0].astype(F32) * inv_ref[...]
    cos, sin = jnp.cos(ang), jnp.sin(ang)
    d = lax.broadcasted_iota(jnp.int32, ang.shape, 1) & (HEAD_DIM - 1)
    half = ROPE_DIM // 2
    c_ref[0] = jnp.where(d < ROPE_DIM, cos, 1.0)
    s1_ref[0] = jnp.where(d < half, -sin, 0.0)
    s2_ref[0] = jnp.where((d >= half) & (d < ROPE_DIM), sin, 0.0)


def _rope_tables(positions, ts):
    b, s = positions.shape
    half = ROPE_DIM // 2
    inv = ROPE_THETA ** (-jnp.arange(0, ROPE_DIM, 2, dtype=F32) / ROPE_DIM)
    inv_row = jnp.tile(inv, LANES // half)[None, :]
    tab = jax.ShapeDtypeStruct((b, s, LANES), F32)
    spec = pl.BlockSpec((1, ts, LANES), lambda i, j: (i, j, 0))
    return pl.pallas_call(
        _rope_kernel, grid=(b, s // ts),
        in_specs=[pl.BlockSpec((1, ts, 1), lambda i, j: (i, j, 0)), _const_spec((1, LANES))],
        out_specs=[spec, spec, spec], out_shape=[tab, tab, tab],
        compiler_params=_params(2), name="rope_tables",
    )(positions[..., None], inv_row)


def _ada_kernel(c_ref, w_ref, b_ref, o_ref):
    c = c_ref[...]
    act = c * jax.nn.sigmoid(c)
    o_ref[...] = jnp.dot(act, w_ref[...], preferred_element_type=F32) + b_ref[...]


def _ada_all(c, w_ada, b_ada, tn=2048):
    depth, d, n = w_ada.shape
    b = c.shape[0]
    rows = -(-b // 8) * 8
    c_pad = jnp.pad(c, ((0, rows - b), (0, 0)))
    out = pl.pallas_call(
        _ada_kernel, grid=(depth, n // tn),
        in_specs=[pl.BlockSpec((rows, d), lambda l, j: (0, 0)),
                  pl.BlockSpec((None, d, tn), lambda l, j: (l, 0, j)),
                  pl.BlockSpec((None, 1, tn), lambda l, j: (l, 0, j))],
        out_specs=pl.BlockSpec((None, rows, tn), lambda l, j: (l, 0, j)),
        out_shape=jax.ShapeDtypeStruct((depth, rows, n), F32),
        compiler_params=_params(2), name="ada_cond",
    )(c_pad, w_ada, b_ada[:, None, :])
    return out[:, :b].reshape(depth, b, 6, d)


def _inproj_kernel(x_ref, ada_ref, w_ref, b_ref, wf_ref, bf_ref, wpool_ref, pscale_ref, convw_ref,
                   rc_ref, rs1_ref, rs2_ref, tri_ref,
                   u_ref, ypool_ref, yconv_ref, qm_ref, km_ref, vm_ref, qf_ref, kf_ref, vf_ref,
                   pbuf, vbuf, ccarry, *, ts):
    t = pl.program_id(1)

    @pl.when(t == 0)
    def _():
        pbuf[0:POOL_HALO, :] = jnp.zeros((POOL_HALO, BRANCH), F32)
        vbuf[0:CONV_HALO, :] = jnp.zeros((CONV_HALO, BRANCH), F32)
        ccarry[...] = jnp.zeros_like(ccarry)

    n = min(GROUP_ROWS, ts)

    def row_group(r0):
        rows = slice(r0, r0 + n)
        u = _layer_norm(x_ref[0, rows, :]) * (1.0 + ada_ref[0, 1:2, :]) + ada_ref[0, 0:1, :]
        ub = u.astype(BF16)
        u_ref[0, rows, :] = ub

        def proj(lo, hi):
            return jnp.dot(ub, w_ref[:, lo:hi], preferred_element_type=F32) + b_ref[:, lo:hi]

        lane = lax.broadcasted_iota(jnp.int32, (n, LANES), 1)
        row = lax.broadcasted_iota(jnp.int32, (n, 1), 0)
        first = lane < HEAD_DIM

        p_in = proj(0, BRANCH)
        pbuf[POOL_HALO:POOL_HALO + n, :] = p_in
        posp1 = (t * ts + r0 + row + 1).astype(F32)

        def window_sums(lo):
            sums, acc = {}, pbuf[0:POOL_HALO + n, lo:lo + LANES]
            for w in (1, 2, 4, 8):
                acc = acc + pltpu.roll(acc, w, 0)
                sums[2 * w] = acc[POOL_HALO:, :]
            return sums

        pooled = []
        for pair in range(2):
            w_a, w_b = POOL_WINDOWS[2 * pair], POOL_WINDOWS[2 * pair + 1]
            lo = pair * LANES
            sums = window_sums(lo)
            win = jnp.where(first, sums[w_a], sums[w_b])
            cnt = jnp.where(first, jnp.minimum(posp1, float(w_a)), jnp.minimum(posp1, float(w_b)))
            pooled.append(win / cnt - p_in[:, lo:lo + LANES])
        pooled = jnp.concatenate(pooled, axis=-1).astype(BF16)
        y_pool = jnp.dot(pooled, wpool_ref[...], preferred_element_type=F32) * pscale_ref[...]
        ypool_ref[0, rows, :] = y_pool.astype(BF16)
        pbuf[0:POOL_HALO, :] = pbuf[n:n + POOL_HALO, :]

        conv = proj(BRANCH, 4 * BRANCH)
        v = conv[:, BRANCH:2 * BRANCH] * conv[:, 2 * BRANCH:3 * BRANCH]
        vbuf[CONV_HALO:CONV_HALO + n, :] = v
        v_hist = vbuf[0:CONV_HALO + n, :]
        z = (convw_ref[2:3, :] * v
             + convw_ref[1:2, :] * pltpu.roll(v_hist, 1, 0)[CONV_HALO:, :]
             + convw_ref[0:1, :] * pltpu.roll(v_hist, 2, 0)[CONV_HALO:, :])
        yconv_ref[0, rows, :] = (conv[:, 0:BRANCH] * z).astype(BF16)
        vbuf[0:CONV_HALO, :] = vbuf[n:n + CONV_HALO, :]

        rc, rs1, rs2 = rc_ref[0, rows, :], rs1_ref[0, rows, :], rs2_ref[0, rows, :]
        half = ROPE_DIM // 2

        def rope_pairs(a):
            out = []
            for pair in range(2):
                ah = a[:, pair * LANES:(pair + 1) * LANES]
                out.append(ah * rc + pltpu.roll(ah, LANES - half, 1) * rs1 + pltpu.roll(ah, half, 1) * rs2)
            return out

        def plain_pairs(a):
            return [a[:, 0:LANES], a[:, LANES:2 * LANES]]

        def head(pairs, h):
            a = pairs[h // 2]
            return pltpu.roll(a, HEAD_DIM, 1) if h % 2 else a

        ones_col = jnp.where(lane == HEAD_DIM, 1.0, 0.0)

        moba = proj(4 * BRANCH, 7 * BRANCH)
        mq = rope_pairs(moba[:, 0:BRANCH])
        mk = rope_pairs(moba[:, BRANCH:2 * BRANCH])
        mv = plain_pairs(moba[:, 2 * BRANCH:3 * BRANCH])
        blk = (t * ts + r0 + row) // MOBA_BLOCK
        blk_onehot = jnp.where(lane == HEAD_DIM + blk, 1.0, 0.0)
        for h in range(N_HEADS):
            qm_ref[0, h, 0, :, rows] = jnp.where(first, head(mq, h) * QSCALE, 0.0).T.astype(BF16)
            km_ref[0, h, rows, :] = jnp.where(first, head(mk, h), blk_onehot).astype(BF16)
            vm_ref[0, h, 0, :, rows] = jnp.where(first, head(mv, h), ones_col).T.astype(BF16)

        fl = jnp.dot(ub, wf_ref[...], preferred_element_type=F32) + bf_ref[...]
        log_f = jnp.minimum(fl, 0.0) - jnp.log1p(jnp.exp(-jnp.abs(fl)))
        hi, mid, lo = _split3(log_f)
        zero = jnp.zeros_like(hi)
        pieces = jnp.where(lane < N_HEADS, hi, jnp.where(lane < 2 * N_HEADS, mid,
                                                          jnp.where(lane < 3 * N_HEADS, lo, zero)))
        cum = jnp.dot(tri_ref[0:n, 0:n], pieces, preferred_element_type=F32)

        fox = proj(7 * BRANCH, 10 * BRANCH)
        fq = plain_pairs(fox[:, 0:BRANCH])
        fk = plain_pairs(fox[:, BRANCH:2 * BRANCH])
        fv = plain_pairs(fox[:, 2 * BRANCH:3 * BRANCH])
        one_q = jnp.where((lane >= HEAD_DIM + 3) & (lane < HEAD_DIM + 6), 1.0, 0.0)
        one_k = jnp.where((lane >= HEAD_DIM) & (lane < HEAD_DIM + 3), 1.0, 0.0)
        for h in range(N_HEADS):
            mine = (lane < 3 * N_HEADS) & ((lane & (N_HEADS - 1)) == h)
            c_h = jnp.sum(jnp.where(mine, cum, 0.0), axis=-1, keepdims=True) + ccarry[h:h + 1, 0:1]
            ccarry[h:h + 1, :] = jnp.broadcast_to(c_h[n - 1:n, :], (1, LANES))
            c_hi, c_mid, c_lo = (p.astype(F32) for p in _split3(c_h * LOG2E))
            cq = jnp.where(lane == HEAD_DIM, c_hi, jnp.where(lane == HEAD_DIM + 1, c_mid,
                                                              jnp.where(lane == HEAD_DIM + 2, c_lo, one_q)))
            ck = jnp.where(lane == HEAD_DIM + 3, -c_hi, jnp.where(lane == HEAD_DIM + 4, -c_mid,
                                                                   jnp.where(lane == HEAD_DIM + 5, -c_lo, one_k)))
            qf_ref[0, h, 0, :, rows] = jnp.where(first, head(fq, h) * QSCALE, cq).T.astype(BF16)
            kf_ref[0, h, rows, :] = jnp.where(first, head(fk, h), ck).astype(BF16)
            vf_ref[0, h, 0, :, rows] = jnp.where(first, head(fv, h), ones_col).T.astype(BF16)

    for r in range(ts // n):
        row_group(r * n)


def _inproj(x, ada, w_main, b_main, w_fgt, b_fgt, w_poolbd, pool_scale, conv_w, rope, tri, ts, layer):
    b, s, d = x.shape
    tok = lambda width: pl.BlockSpec((1, ts, width), lambda i, j: (i, j, 0))
    heads = pl.BlockSpec((1, N_HEADS, ts, LANES), lambda i, j: (i, 0, j, 0))
    heads_shape = jax.ShapeDtypeStruct((b, N_HEADS, s, LANES), BF16)
    heads_t = pl.BlockSpec((1, N_HEADS, 1, LANES, ts), lambda i, j: (i, 0, j, 0, 0))
    heads_t_shape = jax.ShapeDtypeStruct((b, N_HEADS, s // ts, LANES, ts), BF16)
    branch_shape = jax.ShapeDtypeStruct((b, s, BRANCH), BF16)
    return pl.pallas_call(
        functools.partial(_inproj_kernel, ts=ts), grid=(b, s // ts),
        in_specs=[tok(d), _ada_spec(ada, layer)]
                 + [_layer_spec(w, layer) for w in (w_main, b_main, w_fgt, b_fgt, w_poolbd, pool_scale, conv_w)]
                 + [tok(LANES), tok(LANES), tok(LANES), _const_spec((ts, ts))],
        out_specs=[tok(d), tok(BRANCH), tok(BRANCH)] + [heads_t, heads, heads_t] * 2,
        out_shape=[jax.ShapeDtypeStruct((b, s, d), BF16), branch_shape, branch_shape]
                  + [heads_t_shape, heads_shape, heads_t_shape] * 2,
        scratch_shapes=[pltpu.VMEM((ts + POOL_HALO, BRANCH), F32),
                        pltpu.VMEM((ts + CONV_HALO, BRANCH), F32),
                        pltpu.VMEM((8, LANES), F32)],
        compiler_params=_params(2), name="inproj_mixers",
    )(x, ada, w_main, b_main, w_fgt, b_fgt, w_poolbd, pool_scale, conv_w, *rope, tri)


def _attn_kernel(q_ref, k_ref, vt_ref, o_ref, qt_scr, m_scr, acc_scr, sa_scr, sb_scr, *rest, tq, moba):
    i = pl.program_id(1)
    seq = k_ref.shape[2]
    n_blk = seq // MOBA_BLOCK
    nb_pad = -(-n_blk // 8) * 8

    if moba:
        kbar_scr, kb_scr = rest

        @pl.when(i == 0)
        def _():
            kbar_scr[...] = jnp.zeros_like(kbar_scr)
            lane1 = lax.broadcasted_iota(jnp.int32, (1, LANES), 1)
            for h in range(N_HEADS):
                def block_mean(n, carry):
                    start = pl.multiple_of(n * MOBA_BLOCK, MOBA_BLOCK)
                    kb = k_ref[0, h, pl.ds(start, MOBA_BLOCK), :].astype(F32)
                    mean = jnp.sum(kb, axis=0, keepdims=True) * (1.0 / MOBA_BLOCK)
                    kbar_scr[h, pl.ds(HEAD_DIM + n, 1), :] = jnp.where(lane1 < HEAD_DIM, mean, 0.0)
                    return carry
                lax.fori_loop(0, n_blk, block_mean, 0)
                for piece, val in enumerate(_split3(kbar_scr[h])):
                    kb_scr[h, piece] = val

    key = lax.broadcasted_iota(jnp.int32, (tq, tq), 0)
    qry = lax.broadcasted_iota(jnp.int32, (tq, tq), 1)
    causal = key <= qry

    if moba:
        for h in range(N_HEADS):
            qt_b = q_ref[0, h, 0]
            sc = None
            for piece in range(3):
                part = jnp.dot(kb_scr[h, piece], qt_b, preferred_element_type=F32)
                sc = part if sc is None else sc + part
            sc = sc[HEAD_DIM:HEAD_DIM + nb_pad, :]
            n_id = lax.broadcasted_iota(jnp.int32, sc.shape, 0).astype(F32)
            own = ((i * tq + lax.broadcasted_iota(jnp.int32, sc.shape, 1)) // MOBA_BLOCK).astype(F32)
            sc = jnp.where(n_id < own, sc, NEG)
            keep = n_id == own
            for _ in range(MOBA_TOPK):
                best = jnp.max(sc, axis=0, keepdims=True)
                cand = (sc == best) & (sc > NEG)
                pick = n_id == jnp.min(jnp.where(cand, n_id, float(LANES)), axis=0, keepdims=True)
                keep = keep | pick
                sc = jnp.where(pick, NEG, sc)
            bias_t = jnp.where(keep, 0.0, MASK)
            rows = [jnp.zeros((HEAD_DIM, tq), F32), bias_t]
            if nb_pad < LANES - HEAD_DIM:
                rows.append(jnp.zeros((LANES - HEAD_DIM - nb_pad, tq), F32))
            qt_scr[h] = qt_b + jnp.concatenate(rows, axis=0).astype(BF16)

    def q_t(h):
        return qt_scr[h] if moba else q_ref[0, h, 0]

    m_scr[...] = jnp.full_like(m_scr, MASK)
    acc_scr[...] = jnp.zeros_like(acc_scr)

    def scores(j, s_scr):
        start = pl.multiple_of(j * tq, tq)
        for h in range(N_HEADS):
            s_scr[h] = jnp.dot(k_ref[0, h, pl.ds(start, tq), :], q_t(h),
                               preferred_element_type=F32)

    def consume(j, s_scr, diag):
        for h in range(N_HEADS):
            st = s_scr[h]
            if diag:
                st = jnp.where(causal, st, MASK)
            m_old = m_scr[h]
            m_new = jnp.maximum(m_old, jnp.max(st, axis=0, keepdims=True))
            pt = jnp.exp2(st - m_new)
            acc_scr[h] = acc_scr[h] * jnp.exp2(m_old - m_new) + jnp.dot(
                vt_ref[0, h, j], pt.astype(BF16), preferred_element_type=F32)
            m_scr[h] = m_new

    scores(0, sa_scr)

    def pair(jj, carry):
        j = 2 * jj
        scores(j + 1, sb_scr)
        consume(j, sa_scr, False)
        scores(j + 2, sa_scr)
        consume(j + 1, sb_scr, False)
        return carry

    lax.fori_loop(0, i // 2, pair, 0)

    @pl.when(i % 2 == 0)
    def _():
        consume(i, sa_scr, True)

    @pl.when(i % 2 == 1)
    def _():
        scores(i, sb_scr)
        consume(i - 1, sa_scr, False)
        consume(i, sb_scr, True)

    for h in range(N_HEADS):
        acc = acc_scr[h]
        out = acc[0:HEAD_DIM, :] / acc[HEAD_DIM:HEAD_DIM + 1, :]
        o_ref[0, h * HEAD_DIM:(h + 1) * HEAD_DIM, :] = out.astype(BF16)


def _attention(qt, k, vt, tq, moba):
    b, nh, s, _ = k.shape
    assert qt.shape == vt.shape == (b, nh, s // tq, LANES, tq)
    whole = pl.BlockSpec((1, nh, s, LANES), lambda i, j: (i, 0, 0, 0), pipeline_mode=pl.Buffered(1))
    whole_t = pl.BlockSpec((1,) + vt.shape[1:], lambda i, j: (i, 0, 0, 0, 0), pipeline_mode=pl.Buffered(1))
    scratch = [pltpu.VMEM((nh, LANES, tq), BF16), pltpu.VMEM((nh, 1, tq), F32),
               pltpu.VMEM((nh, LANES, tq), F32),
               pltpu.VMEM((nh, tq, tq), F32), pltpu.VMEM((nh, tq, tq), F32)]
    if moba:
        scratch += [pltpu.VMEM((nh, LANES, LANES), F32), pltpu.VMEM((nh, 3, LANES, LANES), BF16)]
    return pl.pallas_call(
        functools.partial(_attn_kernel, tq=tq, moba=moba), grid=(b, s // tq),
        in_specs=[pl.BlockSpec((1, nh, 1, LANES, tq), lambda i, j: (i, 0, j, 0, 0)), whole, whole_t],
        out_specs=pl.BlockSpec((1, BRANCH, tq), lambda i, j: (i, 0, j)),
        out_shape=jax.ShapeDtypeStruct((b, BRANCH, s), BF16),
        scratch_shapes=scratch,
        compiler_params=_params(2), name="moba_attn" if moba else "fox_attn",
    )(qt, k, vt)


def _merge_kernel(x_ref, ada_ref, u_ref, yp_ref, yc_ref, ym_ref, yf_ref,
                  wg_ref, bg_ref, wbr_ref, wo_ref, lng_ref, lnb_ref, o_ref, *, alpha):
    d = x_ref.shape[-1]
    rows = min(GROUP_ROWS, x_ref.shape[1])
    for r in range(x_ref.shape[1] // rows):
        tok = slice(r * rows, (r + 1) * rows)
        ub = u_ref[0, tok, :]
        merged = None
        for n, y_ref in enumerate((yp_ref, yc_ref, ym_ref, yf_ref)):
            gate = jax.nn.sigmoid(jnp.dot(ub, wg_ref[:, n * d:(n + 1) * d], preferred_element_type=F32)
                                  + bg_ref[:, n * d:(n + 1) * d])
            if n < 2:
                branch = jnp.dot(y_ref[0, tok, :], wbr_ref[n], preferred_element_type=F32)
            else:
                branch = lax.dot_general(y_ref[0, :, tok], wbr_ref[n], _TN, preferred_element_type=F32)
            term = gate * branch
            merged = term if merged is None else merged + term
        mixed = jnp.dot(merged.astype(BF16), wo_ref[...], preferred_element_type=F32)
        res = alpha * x_ref[0, tok, :] + ada_ref[0, 2:3, :] * mixed
        o_ref[0, tok, :] = _layer_norm(res) * lng_ref[...] + lnb_ref[...]


def _merge(x, ada, u, ys, w_gate, b_gate, w_branch, w_o, ln_g, ln_b, ts, alpha, layer):
    b, s, d = x.shape
    tok = lambda width: pl.BlockSpec((1, ts, width), lambda i, j: (i, j, 0))
    return pl.pallas_call(
        functools.partial(_merge_kernel, alpha=alpha), grid=(b, s // ts),
        in_specs=[tok(d), _ada_spec(ada, layer), tok(d), tok(BRANCH), tok(BRANCH)]
                 + [pl.BlockSpec((1, BRANCH, ts), lambda i, j: (i, 0, j))] * 2
                 + [_layer_spec(w, layer) for w in (w_gate, b_gate, w_branch, w_o, ln_g, ln_b)],
        out_specs=tok(d), out_shape=jax.ShapeDtypeStruct((b, s, d), F32),
        compiler_params=_params(2), name="merge_out",
    )(x, ada, u, *ys, w_gate, b_gate, w_branch, w_o, ln_g, ln_b)


def _mlp_kernel(x_ref, ada_ref, wup_ref, bup_ref, wdn_ref, lng_ref, lnb_ref, o_ref, *, alpha, chunk):
    rows = min(GROUP_ROWS, x_ref.shape[1])
    for r in range(x_ref.shape[1] // rows):
        x = x_ref[0, r * rows:(r + 1) * rows, :]
        u2 = (_layer_norm(x) * (1.0 + ada_ref[0, 4:5, :]) + ada_ref[0, 3:4, :]).astype(BF16)
        y = None
        for lo in range(0, wup_ref.shape[1], chunk):
            hid = jnp.dot(u2, wup_ref[:, lo:lo + chunk], preferred_element_type=F32) + bup_ref[:, lo:lo + chunk]
            hid = jnp.square(jnp.maximum(hid, 0.0)).astype(BF16)
            part = jnp.dot(hid, wdn_ref[lo:lo + chunk, :], preferred_element_type=F32)
            y = part if y is None else y + part
        res = alpha * x + ada_ref[0, 5:6, :] * y
        o_ref[0, r * rows:(r + 1) * rows, :] = _layer_norm(res) * lng_ref[...] + lnb_ref[...]


def _mlp(x, ada, w_up, b_up, w_down, ln_g, ln_b, ts, alpha, layer, chunk=1024):
    b, s, d = x.shape
    tok = pl.BlockSpec((1, ts, d), lambda i, j: (i, j, 0))
    return pl.pallas_call(
        functools.partial(_mlp_kernel, alpha=alpha, chunk=chunk), grid=(b, s // ts),
        in_specs=[tok, _ada_spec(ada, layer)]
                 + [_layer_spec(w, layer) for w in (w_up, b_up, w_down, ln_g, ln_b)],
        out_specs=tok, out_shape=jax.ShapeDtypeStruct((b, s, d), F32),
        compiler_params=_params(2), name="mlp",
    )(x, ada, w_up, b_up, w_down, ln_g, ln_b)


def _merge_mlp_kernel(x_ref, ada_ref, u_ref, yp_ref, yc_ref, ym_ref, yf_ref, wg_ref, bg_ref, wbr_ref, wo_ref,
                      ln1g_ref, ln1b_ref, wup_ref, bup_ref, wdn_ref, ln2g_ref, ln2b_ref, o_ref, *, alpha, chunk):
    d = x_ref.shape[-1]
    rows = min(GROUP_ROWS, x_ref.shape[1])
    for r in range(x_ref.shape[1] // rows):
        tok = slice(r * rows, (r + 1) * rows)
        ub = u_ref[0, tok, :]
        merged = None
        for n, y_ref in enumerate((yp_ref, yc_ref, ym_ref, yf_ref)):
            gate = jax.nn.sigmoid(jnp.dot(ub, wg_ref[:, n * d:(n + 1) * d], preferred_element_type=F32)
                                  + bg_ref[:, n * d:(n + 1) * d])
            if n < 2:
                branch = jnp.dot(y_ref[0, tok, :], wbr_ref[n], preferred_element_type=F32)
            else:
                branch = lax.dot_general(y_ref[0, :, tok], wbr_ref[n], _TN, preferred_element_type=F32)
            term = gate * branch
            merged = term if merged is None else merged + term
        mixed = jnp.dot(merged.astype(BF16), wo_ref[...], preferred_element_type=F32)
        x1 = _layer_norm(alpha * x_ref[0, tok, :] + ada_ref[0, 2:3, :] * mixed) * ln1g_ref[...] + ln1b_ref[...]

        u2 = (_layer_norm(x1) * (1.0 + ada_ref[0, 4:5, :]) + ada_ref[0, 3:4, :]).astype(BF16)
        y = None
        for lo in range(0, wup_ref.shape[1], chunk):
            hid = jnp.dot(u2, wup_ref[:, lo:lo + chunk], preferred_element_type=F32) + bup_ref[:, lo:lo + chunk]
            hid = jnp.square(jnp.maximum(hid, 0.0)).astype(BF16)
            part = jnp.dot(hid, wdn_ref[lo:lo + chunk, :], preferred_element_type=F32)
            y = part if y is None else y + part
        res = alpha * x1 + ada_ref[0, 5:6, :] * y
        o_ref[0, tok, :] = _layer_norm(res) * ln2g_ref[...] + ln2b_ref[...]


def _merge_mlp(x, ada, u, ys, weights, ts, alpha, layer, chunk=1024):
    b, s, d = x.shape
    tok = lambda width: pl.BlockSpec((1, ts, width), lambda i, j: (i, j, 0))
    return pl.pallas_call(
        functools.partial(_merge_mlp_kernel, alpha=alpha, chunk=chunk), grid=(b, s // ts),
        in_specs=[tok(d), _ada_spec(ada, layer), tok(d), tok(BRANCH), tok(BRANCH)]
                 + [pl.BlockSpec((1, BRANCH, ts), lambda i, j: (i, 0, j))] * 2
                 + [_layer_spec(w, layer) for w in weights],
        out_specs=tok(d), out_shape=jax.ShapeDtypeStruct((b, s, d), F32),
        compiler_params=_params(2), name="merge_mlp",
    )(x, ada, u, *ys, *weights)


def _tile(s):
    for ts in (512, 256):
        if s % ts == 0:
            return ts
    raise ValueError(f"sequence length {s} must be a multiple of {MOBA_BLOCK}")


def kernel(x, c, positions, w_ada, b_ada, w_in, b_in, w_pool, pool_scale, conv_w, w_branch, w_o,
           ln1_g, ln1_b, w_up, b_up, w_down, ln2_g, ln2_b):
    depth = w_ada.shape[0]
    b, s, d = x.shape
    assert d % LANES == 0 and w_in.shape[2] == N_MAIN + N_HEADS + N_HEADS * d
    assert s // MOBA_BLOCK <= LANES - HEAD_DIM
    ts = _tile(s)
    ts_wide = 2 * ts if s % (2 * ts) == 0 else ts
    alpha = float((2 * depth) ** 0.25)

    rope = _rope_tables(positions, ts)
    ada_all = _ada_all(c, w_ada, b_ada)
    tri = jnp.tril(jnp.ones((ts, ts), BF16))

    gate0 = N_MAIN + N_HEADS
    pad = LANES - 3 * N_HEADS
    w_f, b_f = w_in[:, :, N_MAIN:gate0], b_in[:, N_MAIN:gate0]
    w_main = w_in[:, :, :N_MAIN].astype(BF16)
    b_main = b_in[:, None, :N_MAIN]
    w_fgt = jnp.concatenate([w_f, w_f, w_f, jnp.zeros((depth, d, pad), F32)], axis=2).astype(BF16)
    b_fgt = jnp.concatenate([b_f, b_f, b_f, jnp.zeros((depth, pad), F32)], axis=1)[:, None, :]
    w_gate = w_in[:, :, gate0:].astype(BF16)
    b_gate = b_in[:, None, gate0:]
    w_poolbd = jnp.stack([jax.scipy.linalg.block_diag(*[w_pool[l, g] for g in range(w_pool.shape[1])])
                          for l in range(depth)]).astype(BF16)
    row3 = lambda a: a[:, None, :]
    w_branch_b, w_o_b, w_up_b, w_down_b = (w.astype(BF16) for w in (w_branch, w_o, w_up, w_down))

    for l in range(depth):
        u, y_pool, y_conv, qm, km, vm, qf, kf, vf = _inproj(
            x, ada_all, w_main, b_main, w_fgt, b_fgt, w_poolbd, row3(pool_scale), conv_w, rope, tri, ts, l)
        y_moba = _attention(qm, km, vm, ts, moba=True)
        y_fox = _attention(qf, kf, vf, ts, moba=False)
        x = _merge_mlp(x, ada_all, u, (y_pool, y_conv, y_moba, y_fox),
                       (w_gate, b_gate, w_branch_b, w_o_b, row3(ln1_g), row3(ln1_b),
                        w_up_b, row3(b_up), w_down_b, row3(ln2_g), row3(ln2_b)), ts, alpha, l)
    return x
```
